```python
import math
import jax, jax.numpy as jnp
from jax import lax
import numpy as np

D_MODEL = 1024
BATCH = 32
SEQ = 2048
DEPTH = 2

N_META = 16
N_EVEN = (DEPTH + 1) // 2
N_ODD = DEPTH // 2
EPS = 1e-5

D_SSD = D_MODEL // 2
SSD_HEADDIM = 64
SSD_HEADS = D_SSD // SSD_HEADDIM
SSD_GROUPS = 2
SSD_STATE = 128
SSD_CONV = 5
SSD_CHUNK = 128
SSD_CONV_DIM = D_SSD + 2 * SSD_GROUPS * SSD_STATE
D_DIFF = D_MODEL // 2
DIFF_HEADDIM = 64
DIFF_HEADS = D_DIFF // (2 * DIFF_HEADDIM)
ROPE_THETA = 500000.0
ROPE_DIM = DIFF_HEADDIM // 4
Q_BLOCK = 128
IN_AB = D_SSD + SSD_CONV_DIM + 2 * SSD_HEADS + 3 * D_DIFF

D_HYENA = D_MODEL
HYENA_SHORT = 3
HYENA_BANDS = 16
HYENA_EMB = 2 * HYENA_BANDS + 1
HYENA_FILTER_WIDTH = 64
HYENA_SHIFT = 0.05
HYENA_TARGET = 1e-2
HYENA_FAST_DECAY = 0.3
HYENA_SLOW_DECAY = 1.5

D_FF = 2816
N_EXPERTS = 8
TOP_K = 2
D_FF_EXPERT = 3584

kernel_name = 'hybrid_ssd_diffattn_hyena_moe_encoder'

F32 = jnp.float32


def rmsnorm(x, w):
    xf = x.astype(F32)
    y = xf * lax.rsqrt(jnp.mean(xf * xf, axis=-1, keepdims=True) + EPS)
    return y.astype(x.dtype) * w.astype(x.dtype)


def conv_centred(x, w, b):
    K, C = w.shape
    y = lax.conv_general_dilated(x, w[:, None, :].astype(x.dtype), window_strides=(1,),
                                 padding=[(K // 2, K // 2)],
                                 dimension_numbers=('NWC', 'WIO', 'NWC'),
                                 feature_group_count=C)
    return y + b.astype(x.dtype)


def swiglu(x, wg, wu, wd):
    return (jax.nn.silu(x @ wg) * (x @ wu)) @ wd


def rotary_tables(T):
    pos = jnp.arange(T, dtype=F32)
    inv_freq = ROPE_THETA ** (-jnp.arange(0, ROPE_DIM, 2, dtype=F32) / ROPE_DIM)
    ang = pos[:, None] * inv_freq[None, :]
    return jnp.cos(ang), jnp.sin(ang)


def partial_rotary(x, cos, sin):
    half = ROPE_DIM // 2
    c = cos[:, None, None, :].astype(x.dtype)
    s = sin[:, None, None, :].astype(x.dtype)
    x1 = x[..., :half]
    x2 = x[..., half:ROPE_DIM]
    return jnp.concatenate([x1 * c - x2 * s, x2 * c + x1 * s, x[..., ROPE_DIM:]], axis=-1)


def ssd_chunked(xs, dt, A, bm, cm):
    b, L, _, _ = xs.shape
    c = L // SSD_CHUNK
    r = SSD_HEADS // SSD_GROUPS
    xdt = (xs.astype(F32) * dt[..., None]).reshape(b, c, SSD_CHUNK, SSD_GROUPS, r, SSD_HEADDIM)
    a = (dt * A).reshape(b, c, SSD_CHUNK, SSD_GROUPS, r)
    bc = bm.astype(F32).reshape(b, c, SSD_CHUNK, SSD_GROUPS, SSD_STATE)
    cc = cm.astype(F32).reshape(b, c, SSD_CHUNK, SSD_GROUPS, SSD_STATE)
    a_cs = jnp.cumsum(a, axis=2)
    a_t = jnp.moveaxis(a_cs, 2, -1)
    seg = a_t[..., :, None] - a_t[..., None, :]
    lower = jnp.tril(jnp.ones((SSD_CHUNK, SSD_CHUNK), dtype=bool))
    lmat = jnp.exp(jnp.where(lower, seg, -jnp.inf))
    cb = jnp.einsum('bcign,bcjgn->bcgij', cc, bc)
    y_diag = jnp.einsum('bcgrij,bcjgrp->bcigrp', cb[:, :, :, None] * lmat, xdt)
    decay_states = jnp.exp(a_cs[:, :, -1:] - a_cs)
    states = jnp.einsum('bcjgn,bcjgrp->bcgrpn', bc, xdt * decay_states[..., None])
    chunk_decay = jnp.exp(a_cs[:, :, -1])

    def step(h, inp):
        st, dec = inp
        return h * dec[..., None, None] + st, h

    _, prev = lax.scan(step, jnp.zeros_like(states[:, 0]),
                       (jnp.moveaxis(states, 1, 0), jnp.moveaxis(chunk_decay, 1, 0)))
    y_off = jnp.einsum('bcign,cbgrpn->bcigrp', cc, prev) * jnp.exp(a_cs)[..., None]
    return (y_diag + y_off).reshape(b, L, SSD_HEADS, SSD_HEADDIM).astype(xs.dtype)


def ssd_direction(xs, dt, A, bm, cm, reverse):
    T = xs.shape[1]
    pad = SSD_CHUNK - N_META
    if reverse:
        xs, dt, bm, cm = [jnp.flip(t, axis=1) for t in (xs, dt, bm, cm)]
        widths = (0, pad)
    else:
        widths = (pad, 0)
    padf = lambda t: jnp.pad(t, [(0, 0), widths] + [(0, 0)] * (t.ndim - 2))
    y = ssd_chunked(padf(xs), padf(dt), A, padf(bm), padf(cm))
    y = y[:, widths[0]:widths[0] + T]
    return jnp.flip(y, axis=1) if reverse else y


def ssd_mixer(z, xbc, dt_raw, conv_w, conv_b, a_log, dt_bias, d_skip, norm_w):
    b, T, _ = xbc.shape
    xbc = jax.nn.silu(conv_centred(xbc, conv_w, conv_b))
    gn = SSD_GROUPS * SSD_STATE
    xs = xbc[..., :D_SSD].reshape(b, T, SSD_HEADS, SSD_HEADDIM)
    bm = xbc[..., D_SSD:D_SSD + gn].reshape(b, T, SSD_GROUPS, SSD_STATE)
    cm = xbc[..., D_SSD + gn:].reshape(b, T, SSD_GROUPS, SSD_STATE)
    y = (d_skip[0] + d_skip[1]).astype(xs.dtype)[:, None] * xs
    for direction in range(2):
        sl = slice(direction * SSD_HEADS, (direction + 1) * SSD_HEADS)
        dt = jax.nn.softplus(dt_raw[..., sl].astype(F32) + dt_bias[direction].astype(F32))
        A = -jnp.exp(a_log[direction].astype(F32))
        y = y + ssd_direction(xs, dt, A, bm, cm, reverse=(direction == 1))
    y = y.reshape(b, T, D_SSD) * jax.nn.silu(z)
    yg = y.reshape(b, T, SSD_GROUPS, D_SSD // SSD_GROUPS)
    return rmsnorm(yg, norm_w.reshape(SSD_GROUPS, -1)).reshape(b, T, D_SSD)


def diff_attn_block(qb, k, v, lam):
    s = jnp.einsum('bqhmd,bkhmd->bhmqk', qb, k).astype(F32) * (DIFF_HEADDIM ** -0.5)
    p = jax.nn.softmax(s, axis=-1)
    a = p[:, :, 0] - lam * p[:, :, 1]
    return jnp.einsum('bhqk,bkhe->bqhe', a.astype(v.dtype), v)


def diff_attention(qkv, cos, sin, lam_params, subln_w, lambda_init):
    b, T, _ = qkv.shape
    H, d = DIFF_HEADS, DIFF_HEADDIM
    q = partial_rotary(qkv[..., :D_DIFF].reshape(b, T, H, 2, d), cos, sin)
    k = partial_rotary(qkv[..., D_DIFF:2 * D_DIFF].reshape(b, T, H, 2, d), cos, sin)
    v = qkv[..., 2 * D_DIFF:].reshape(b, T, H, 2 * d)
    lp = lam_params.astype(F32)
    lam = jnp.exp(jnp.sum(lp[0] * lp[1])) - jnp.exp(jnp.sum(lp[2] * lp[3])) + lambda_init
    block = lambda qb: diff_attn_block(qb, k, v, lam)
    o_meta = block(q[:, :N_META])
    n_blk = (T - N_META) // Q_BLOCK
    qr = jnp.moveaxis(q[:, N_META:].reshape(b, n_blk, Q_BLOCK, H, 2, d), 1, 0)
    o_real = lax.map(block, qr)
    o_real = jnp.moveaxis(o_real, 0, 1).reshape(b, T - N_META, H, 2 * d)
    o = jnp.concatenate([o_meta, o_real], axis=1)
    o = rmsnorm(o, subln_w) * (1.0 - lambda_init)
    return o.reshape(b, T, D_DIFF)


def hyena_filters(T, w1, b1, w2, b2, w3, b3, w4, freq):
    pos = jnp.arange(T, dtype=F32)
    t = pos / (T - 1)
    bands = jnp.linspace(1e-4, HYENA_BANDS - 1, HYENA_BANDS, dtype=F32)
    ang = (2.0 * math.pi * pos / T)[:, None] * bands[None, :]
    zpos = jnp.concatenate([t[:, None], jnp.cos(ang), -jnp.sin(ang)], axis=-1)
    fr = freq.astype(F32)
    h = jnp.sin(fr[0] * (zpos @ w1.astype(F32) + b1.astype(F32)))
    h = jnp.sin(fr[1] * (h @ w2.astype(F32) + b2.astype(F32)))
    h = jnp.sin(fr[2] * (h @ w3.astype(F32) + b3.astype(F32)))
    h = h @ w4.astype(F32)
    max_decay = math.log(HYENA_TARGET) / HYENA_FAST_DECAY
    min_decay = math.log(HYENA_TARGET) / HYENA_SLOW_DECAY
    deltas = jnp.abs(jnp.linspace(min_decay, max_decay, D_HYENA, dtype=F32))
    decay = jnp.exp(-t[:, None] * deltas[None, :]) + HYENA_SHIFT
    h = h * jnp.concatenate([decay, decay], axis=-1)
    hf, hb = h[:, :D_HYENA], h[:, D_HYENA:]
    k = jnp.concatenate([hf, jnp.zeros((1, D_HYENA), F32), hb[:0:-1]], axis=0)
    return k / jnp.sum(jnp.abs(k), axis=0, keepdims=True)


def hyena_mixer(u, w_in, b_in, conv_w, conv_b, f_w1, f_b1, f_w2, f_b2, f_w3, f_b3, f_w4,
                f_freq, skip, w_out, b_out):
    b, T, _ = u.shape
    p = conv_centred(u @ w_in + b_in, conv_w, conv_b)
    x0 = p[..., :D_HYENA]
    x1 = p[..., D_HYENA:2 * D_HYENA]
    v = p[..., 2 * D_HYENA:] * x1
    k = hyena_filters(T, f_w1, f_b1, f_w2, f_b2, f_w3, f_b3, f_w4, f_freq)
    fv = jnp.fft.rfft(v.astype(F32), n=2 * T, axis=1)
    fk = jnp.fft.rfft(k, n=2 * T, axis=0)
    y = jnp.fft.irfft(fv * fk[None], n=2 * T, axis=1)[:, :T]
    y = (y + v.astype(F32) * skip.astype(F32)).astype(u.dtype) * x0
    return y @ w_out + b_out


def moe_swiglu(x, router, wg, wu, wd):
    b, T, D = x.shape
    xt = x.reshape(b * T, D)
    logits = (xt @ router).astype(F32)
    top_v, top_i = lax.top_k(logits, TOP_K)
    wts = jax.nn.softmax(top_v, axis=-1)
    gates = jnp.sum(jax.nn.one_hot(top_i, N_EXPERTS, dtype=F32) * wts[..., None], axis=1)
    out = jnp.zeros_like(xt)
    for e in range(N_EXPERTS):
        ye = swiglu(xt, wg[e], wu[e], wd[e])
        out = out + ye * gates[:, e:e + 1].astype(xt.dtype)
    return out.reshape(b, T, D)


def setup_inputs(seed: int = 0) -> dict:
    key = jax.random.key(seed)
    keys = jax.random.split(key, 64)
    counter = [0]

    def nk():
        k = keys[counter[0]]
        counter[0] += 1
        return k

    def nrm(shape, scale):
        return jax.random.normal(nk(), shape, F32) * scale

    def gain(shape):
        return 1.0 + 0.01 * jax.random.normal(nk(), shape, F32)

    E, O, D = N_EVEN, N_ODD, D_MODEL
    a_log = jnp.log(jax.random.uniform(nk(), (E, 2, SSD_HEADS), F32, 1.0, 16.0))
    dt0 = jnp.exp(jax.random.uniform(nk(), (E, 2, SSD_HEADS), F32, math.log(1e-3), math.log(1e-1)))
    dt_bias = dt0 + jnp.log(-jnp.expm1(-dt0))
    return {
        'x': nrm((BATCH, SEQ, D), 1.0),
        'meta_tokens': nrm((N_META, D), 1.0),
        'norm_mix_even': gain((E, D)),
        'w_in_ab': nrm((E, D, IN_AB), D ** -0.5),
        'ssd_conv_w': nrm((E, SSD_CONV, SSD_CONV_DIM), SSD_CONV ** -0.5),
        'ssd_conv_b': nrm((E, SSD_CONV_DIM), 0.01),
        'ssd_a_log': a_log,
        'ssd_dt_bias': dt_bias,
        'ssd_d': gain((E, 2, SSD_HEADS)),
        'ssd_norm_w': gain((E, D_SSD)),
        'diff_lambda': nrm((E, 4, DIFF_HEADDIM), 0.1),
        'diff_subln_w': gain((E, 2 * DIFF_HEADDIM)),
        'w_out_ab': nrm((E, D_SSD + D_DIFF, D), (D_SSD + D_DIFF) ** -0.5),
        'norm_ffn_even': gain((E, D)),
        'ffn_w_gate': nrm((E, D, D_FF), D ** -0.5),
        'ffn_w_up': nrm((E, D, D_FF), D ** -0.5),
        'ffn_w_down': nrm((E, D_FF, D), D_FF ** -0.5),
        'norm_mix_odd': gain((O, D)),
        'hy_w_in': nrm((O, D, 3 * D_HYENA), D ** -0.5),
        'hy_b_in': nrm((O, 3 * D_HYENA), 0.01),
        'hy_conv_w': nrm((O, HYENA_SHORT, 3 * D_HYENA), HYENA_SHORT ** -0.5),
        'hy_conv_b': nrm((O, 3 * D_HYENA), 0.01),
        'hy_f_w1': nrm((O, HYENA_EMB, HYENA_FILTER_WIDTH), HYENA_EMB ** -0.5),
        'hy_f_b1': nrm((O, HYENA_FILTER_WIDTH), 0.02),
        'hy_f_w2': nrm((O, HYENA_FILTER_WIDTH, HYENA_FILTER_WIDTH), HYENA_FILTER_WIDTH ** -0.5),
        'hy_f_b2': nrm((O, HYENA_FILTER_WIDTH), 0.02),
        'hy_f_w3': nrm((O, HYENA_FILTER_WIDTH, HYENA_FILTER_WIDTH), HYENA_FILTER_WIDTH ** -0.5),
        'hy_f_b3': nrm((O, HYENA_FILTER_WIDTH), 0.02),
        'hy_f_w4': nrm((O, HYENA_FILTER_WIDTH, 2 * D_HYENA), HYENA_FILTER_WIDTH ** -0.5),
        'hy_f_freq': gain((O, 3, HYENA_FILTER_WIDTH)),
        'hy_skip': nrm((O, D_HYENA), 0.1),
        'hy_w_out': nrm((O, D_HYENA, D), D_HYENA ** -0.5),
        'hy_b_out': nrm((O, D), 0.01),
        'norm_ffn_odd': gain((O, D)),
        'moe_router': nrm((O, D, N_EXPERTS), D ** -0.5),
        'moe_w_gate': nrm((O, N_EXPERTS, D, D_FF_EXPERT), D ** -0.5),
        'moe_w_up': nrm((O, N_EXPERTS, D, D_FF_EXPERT), D ** -0.5),
        'moe_w_down': nrm((O, N_EXPERTS, D_FF_EXPERT, D), D_FF_EXPERT ** -0.5),
        'final_norm': gain((D,)),
    }


def reference(x, meta_tokens, norm_mix_even, w_in_ab, ssd_conv_w, ssd_conv_b, ssd_a_log,
              ssd_dt_bias, ssd_d, ssd_norm_w, diff_lambda, diff_subln_w, w_out_ab,
              norm_ffn_even, ffn_w_gate, ffn_w_up, ffn_w_down, norm_mix_odd, hy_w_in, hy_b_in,
              hy_conv_w, hy_conv_b, hy_f_w1, hy_f_b1, hy_f_w2, hy_f_b2, hy_f_w3, hy_f_b3,
              hy_f_w4, hy_f_freq, hy_skip, hy_w_out, hy_b_out, norm_ffn_odd, moe_router,
              moe_w_gate, moe_w_up, moe_w_down, final_norm):
    b = x.shape[0]
    meta = jnp.broadcast_to(meta_tokens[None].astype(x.dtype), (b, N_META, D_MODEL))
    h = jnp.concatenate([meta, x], axis=1)
    T = h.shape[1]
    cos, sin = rotary_tables(T)
    o0 = D_SSD
    o1 = o0 + SSD_CONV_DIM
    o2 = o1 + 2 * SSD_HEADS
    for layer in range(DEPTH):
        i = layer // 2
        if layer % 2 == 0:
            proj = rmsnorm(h, norm_mix_even[i]) @ w_in_ab[i]
            ssd_out = ssd_mixer(proj[..., :o0], proj[..., o0:o1], proj[..., o1:o2],
                                ssd_conv_w[i], ssd_conv_b[i], ssd_a_log[i], ssd_dt_bias[i],
                                ssd_d[i], ssd_norm_w[i])
            lambda_init = 0.8 - 0.6 * math.exp(-0.3 * layer)
            attn_out = diff_attention(proj[..., o2:], cos, sin, diff_lambda[i],
                                      diff_subln_w[i], lambda_init)
            h = h + jnp.concatenate([ssd_out, attn_out], axis=-1) @ w_out_ab[i]
            h = h + swiglu(rmsnorm(h, norm_ffn_even[i]), ffn_w_gate[i], ffn_w_up[i], ffn_w_down[i])
        else:
            h = h + hyena_mixer(rmsnorm(h, norm_mix_odd[i]), hy_w_in[i], hy_b_in[i],
                                hy_conv_w[i], hy_conv_b[i], hy_f_w1[i], hy_f_b1[i],
                                hy_f_w2[i], hy_f_b2[i], hy_f_w3[i], hy_f_b3[i], hy_f_w4[i],
                                hy_f_freq[i], hy_skip[i], hy_w_out[i], hy_b_out[i])
            h = h + moe_swiglu(rmsnorm(h, norm_ffn_odd[i]), moe_router[i], moe_w_gate[i],
                               moe_w_up[i], moe_w_down[i])
    return rmsnorm(h, final_norm)[:, N_META:]
```

```python
import functools
import math

import jax
import jax.numpy as jnp
from jax import lax
from jax.experimental import pallas as pl
from jax.experimental.pallas import tpu as pltpu

F32 = jnp.float32
BF16 = jnp.bfloat16

D_MODEL = 1024
N_META = 16
EPS = 1e-5
D_SSD = 512
SSD_HEADDIM = 64
SSD_HEADS = 8
SSD_GROUPS = 2
SSD_STATE = 128
SSD_CONV = 5
CHUNK = 128
SSD_CONV_DIM = D_SSD + 2 * SSD_GROUPS * SSD_STATE
D_DIFF = 512
DIFF_HEADDIM = 64
DIFF_HEADS = 4
ROPE_THETA = 500000.0
ROPE_DIM = 16
HYENA_SHORT = 3
HYENA_BANDS = 16
HYENA_SHIFT = 0.05
HYENA_TARGET = 1e-2
HYENA_FAST_DECAY = 0.3
HYENA_SLOW_DECAY = 1.5
N_EXPERTS = 8
TOP_K = 2

LANES = 128
PAD = CHUNK - N_META
VMEM_LIMIT = 48 * 1024 * 1024


def _cparams(n_axes):
    return pltpu.CompilerParams(dimension_semantics=("arbitrary",) * n_axes,
                                vmem_limit_bytes=VMEM_LIMIT)


def _resident(shape):
    zeros = (0,) * len(shape)
    return pl.BlockSpec(shape, lambda *_: zeros, pipeline_mode=pl.Buffered(1))


def _row_tile(n_rows, preferred):
    assert n_rows % CHUNK == 0
    tm = preferred
    while n_rows % tm:
        tm //= 2
    return tm


def _dot(a, b):
    return jnp.dot(a, b, preferred_element_type=F32)


def _dot_nt(a, b):
    return lax.dot_general(a, b, (((1,), (1,)), ((), ())), preferred_element_type=F32)


def _split3(x):
    hi = x.astype(BF16)
    r = x - hi.astype(F32)
    mid = r.astype(BF16)
    lo = (r - mid.astype(F32)).astype(BF16)
    return hi, mid, lo


def _dot3_right(x, m):
    hi, mid, lo = _split3(x)
    return _dot(hi, m) + _dot(mid, m) + _dot(lo, m)


def _dot3_left(m, x):
    hi, mid, lo = _split3(x)
    return _dot(m, hi) + _dot(m, mid) + _dot(m, lo)


def _rms(x, w):
    return x * lax.rsqrt(jnp.mean(x * x, axis=-1, keepdims=True) + EPS) * w


def _silu(x):
    return x * jax.nn.sigmoid(x)


def _rms_proj_kernel(*refs, n_out, has_bias, col_chunk):
    x_ref, g_ref = refs[:2]
    w_refs = refs[2:2 + n_out]
    b_refs = refs[2 + n_out:2 + 2 * n_out] if has_bias else (None,) * n_out
    o_refs = refs[-n_out:]
    xn = _rms(x_ref[...], g_ref[...]).astype(BF16)
    for w_ref, b_ref, o_ref in zip(w_refs, b_refs, o_refs):
        n = o_ref.shape[-1]
        for c0 in range(0, n, col_chunk):
            c1 = min(n, c0 + col_chunk)
            y = _dot(xn, w_ref[:, c0:c1])
            if b_ref is not None:
                y = y + b_ref[:, c0:c1]
            o_ref[:, c0:c1] = y.astype(o_ref.dtype)


def rms_proj(x, g, ws, bs, out_dtypes, tm=512):
    n_rows, d = x.shape
    tm = _row_tile(n_rows, tm)
    n_out = len(ws)
    has_bias = bs is not None
    in_specs = [pl.BlockSpec((tm, d), lambda i: (i, 0)), _resident((1, d))]
    in_specs += [_resident(w.shape) for w in ws]
    args = [x, g.reshape(1, d)] + list(ws)
    if has_bias:
        in_specs += [_resident(b.shape) for b in bs]
        args += list(bs)
    out_shape = [jax.ShapeDtypeStruct((n_rows, w.shape[1]), dt) for w, dt in zip(ws, out_dtypes)]
    out_specs = [pl.BlockSpec((tm, w.shape[1]), lambda i: (i, 0)) for w in ws]
    return pl.pallas_call(
        functools.partial(_rms_proj_kernel, n_out=n_out, has_bias=has_bias, col_chunk=512),
        grid=(n_rows // tm,), in_specs=in_specs, out_specs=out_specs, out_shape=out_shape,
        compiler_params=_cparams(1), name="rms_proj")(*args)


def _proj_res_kernel(*refs, n_in, has_bias):
    res_ref = refs[0]
    a_refs = refs[1:1 + n_in]
    w_refs = refs[1 + n_in:1 + 2 * n_in]
    b_ref = refs[1 + 2 * n_in] if has_bias else None
    o_ref = refs[-1]
    acc = res_ref[...]
    for a_ref, w_ref in zip(a_refs, w_refs):
        acc = acc + _dot(a_ref[...], w_ref[...])
    if b_ref is not None:
        acc = acc + b_ref[...]
    o_ref[...] = acc


def proj_residual(res, a_list, w_list, bias, tm=512):
    n_rows, d = res.shape
    tm = _row_tile(n_rows, tm)
    n_in = len(a_list)
    has_bias = bias is not None
    in_specs = [pl.BlockSpec((tm, d), lambda i: (i, 0))]
    in_specs += [pl.BlockSpec((tm, a.shape[1]), lambda i: (i, 0)) for a in a_list]
    in_specs += [_resident(w.shape) for w in w_list]
    args = [res] + list(a_list) + list(w_list)
    if has_bias:
        in_specs.append(_resident((1, d)))
        args.append(bias.reshape(1, d))
    return pl.pallas_call(
        functools.partial(_proj_res_kernel, n_in=n_in, has_bias=has_bias),
        grid=(n_rows // tm,), in_specs=in_specs,
        out_specs=pl.BlockSpec((tm, d), lambda i: (i, 0)),
        out_shape=jax.ShapeDtypeStruct((n_rows, d), F32),
        compiler_params=_cparams(1), name="proj_residual")(*args)


def _ffn_kernel(x_ref, g_ref, wg_ref, wu_ref, wd_ref, o_ref, hm_ref, *, tf):
    x = x_ref[...]
    xn = _rms(x, g_ref[...]).astype(BF16)
    f = wg_ref.shape[1]
    for c0 in range(0, f, tf):
        gt = _dot(xn, wg_ref[:, c0:c0 + tf])
        ut = _dot(xn, wu_ref[:, c0:c0 + tf])
        hm_ref[:, c0:c0 + tf] = (_silu(gt) * ut).astype(BF16)
    o_ref[...] = x + _dot(hm_ref[...], wd_ref[...])


def ffn_residual(x, g, wg, wu, wd, tm=512, tf=256):
    n_rows, d = x.shape
    tm = _row_tile(n_rows, tm)
    f = wg.shape[1]
    return pl.pallas_call(
        functools.partial(_ffn_kernel, tf=tf),
        grid=(n_rows // tm,),
        in_specs=[pl.BlockSpec((tm, d), lambda i: (i, 0)), _resident((1, d)),
                  _resident(wg.shape), _resident(wu.shape), _resident(wd.shape)],
        out_specs=pl.BlockSpec((tm, d), lambda i: (i, 0)),
        out_shape=jax.ShapeDtypeStruct((n_rows, d), F32),
        scratch_shapes=[pltpu.VMEM((tm, f), BF16)],
        compiler_params=_cparams(1), name="ffn")(x, g.reshape(1, d), wg, wu, wd)


def _final_norm_kernel(x_ref, g_ref, o_ref):
    seq = o_ref.shape[1]
    for r0 in range(0, seq, 512):
        o_ref[0, r0:r0 + 512, :] = _rms(x_ref[0, CHUNK + r0:CHUNK + r0 + 512, :], g_ref[...])


def final_rmsnorm(h3, g, seq):
    b, tp, d = h3.shape
    assert seq % 512 == 0 and tp == seq + CHUNK
    return pl.pallas_call(
        _final_norm_kernel, grid=(b,),
        in_specs=[pl.BlockSpec((1, tp, d), lambda i: (i, 0, 0)), _resident((1, d))],
        out_specs=pl.BlockSpec((1, seq, d), lambda i: (i, 0, 0)),
        out_shape=jax.ShapeDtypeStruct((b, seq, d), F32),
        compiler_params=_cparams(1), name="final_norm")(h3, g.reshape(1, d))


def _rope_table_kernel(freq_ref, cos_ref, sin_ref):
    tp = cos_ref.shape[0]
    pos = (lax.broadcasted_iota(jnp.int32, (tp, LANES), 0) - PAD).astype(F32)
    lane = lax.broadcasted_iota(jnp.int32, (tp, LANES), 1) % DIFF_HEADDIM
    ang = pos * freq_ref[...]
    half = ROPE_DIM // 2
    rot = lane < ROPE_DIM
    cos_ref[...] = jnp.where(rot, jnp.cos(ang), 1.0)
    sin_ref[...] = jnp.where(lane < half, -jnp.sin(ang), jnp.where(rot, jnp.sin(ang), 0.0))


def rope_tables(tp):
    inv_freq = ROPE_THETA ** (-jnp.arange(0, ROPE_DIM, 2, dtype=F32) / ROPE_DIM)
    per_sub = jnp.concatenate([inv_freq, inv_freq, jnp.zeros((DIFF_HEADDIM - ROPE_DIM,), F32)])
    freq = jnp.concatenate([per_sub, per_sub]).reshape(1, LANES)
    return pl.pallas_call(
        _rope_table_kernel, grid=(1,),
        in_specs=[_resident((1, LANES))],
        out_specs=[pl.BlockSpec((tp, LANES), lambda i: (0, 0))] * 2,
        out_shape=[jax.ShapeDtypeStruct((tp, LANES), F32)] * 2,
        compiler_params=_cparams(1), name="rope_tables")(freq)


def _swap_matrix():
    src = lax.broadcasted_iota(jnp.int32, (LANES, LANES), 0)
    dst = lax.broadcasted_iota(jnp.int32, (LANES, LANES), 1)
    sub = dst % DIFF_HEADDIM
    half = ROPE_DIM // 2
    partner = jnp.where(sub < half, dst + half, jnp.where(sub < ROPE_DIM, dst - half, dst))
    return jnp.where(src == partner, 1.0, 0.0).astype(BF16)


def _rotate(x_bf16, cos, sin_signed, swap):
    return x_bf16.astype(F32) * cos + _dot(x_bf16, swap) * sin_signed


def _diff_attn_kernel(q_ref, k_ref, v_ref, cq_ref, sq_ref, ck_ref, sk_ref, lam_ref, w_ref,
                      o_ref, kr_ref, *, lambda_init):
    tq = q_ref.shape[1]
    tp = k_ref.shape[1]
    swap = _swap_matrix()

    @pl.when(pl.program_id(2) == 0)
    def _():
        kr_ref[...] = _rotate(k_ref[0], ck_ref[...], sk_ref[...], swap).astype(BF16)

    scale = DIFF_HEADDIM ** -0.5
    qr = (_rotate(q_ref[0], cq_ref[...], sq_ref[...], swap) * scale).astype(BF16)
    lane = lax.broadcasted_iota(jnp.int32, (tq, LANES), 1)
    key = lax.broadcasted_iota(jnp.int32, (tq, tp), 1)
    kr = kr_ref[...]
    probs = []
    for m in range(2):
        qm = jnp.where((lane // DIFF_HEADDIM) == m, qr, jnp.zeros_like(qr))
        s = jnp.where(key >= PAD, _dot_nt(qm, kr), -jnp.inf)
        p = jnp.exp(s - jnp.max(s, axis=-1, keepdims=True))
        probs.append((p, 1.0 / jnp.sum(p, axis=-1, keepdims=True)))
    lp = lam_ref[...]
    lam = (jnp.exp(jnp.sum(lp[0:1] * lp[1:2], axis=-1, keepdims=True))
           - jnp.exp(jnp.sum(lp[2:3] * lp[3:4], axis=-1, keepdims=True)) + lambda_init)
    (p0, r0), (p1, r1) = probs
    a = (p0 * r0 - p1 * (lam * r1)).astype(BF16)
    o = _dot(a, v_ref[0])
    o_ref[0] = (_rms(o, w_ref[...]) * (1.0 - lambda_init)).astype(o_ref.dtype)


def diff_attention(qkv3, cos, sin, lam_params, subln_w, lambda_init, tq=272):
    b, tp, _ = qkv3.shape
    hw = 2 * DIFF_HEADDIM
    assert hw == LANES and tp % tq == 0
    return pl.pallas_call(
        functools.partial(_diff_attn_kernel, lambda_init=lambda_init),
        grid=(b, DIFF_HEADS, tp // tq),
        in_specs=[pl.BlockSpec((1, tq, hw), lambda i, h, j: (i, j, h)),
                  pl.BlockSpec((1, tp, hw), lambda i, h, j: (i, 0, DIFF_HEADS + h)),
                  pl.BlockSpec((1, tp, hw), lambda i, h, j: (i, 0, 2 * DIFF_HEADS + h)),
                  pl.BlockSpec((tq, LANES), lambda i, h, j: (j, 0)),
                  pl.BlockSpec((tq, LANES), lambda i, h, j: (j, 0)),
                  _resident((tp, LANES)), _resident((tp, LANES)),
                  _resident(lam_params.shape), _resident((1, hw))],
        out_specs=pl.BlockSpec((1, tq, hw), lambda i, h, j: (i, j, h)),
        out_shape=jax.ShapeDtypeStruct((b, tp, D_DIFF), BF16),
        scratch_shapes=[pltpu.VMEM((tp, hw), BF16)],
        compiler_params=_cparams(3), name="diff_attention")(
            qkv3, qkv3, qkv3, cos, sin, cos, sin, lam_params, subln_w.reshape(1, hw))


def _softplus(x):
    return jnp.maximum(x, 0.0) + jnp.log1p(jnp.exp(-jnp.abs(x)))


def _ssd_kernel(zx_ref, dt_ref, cw_ref, cb_ref, dtb_ref, alog_ref, aexp_ref, dsk_ref, nw_ref,
                o_ref, xact_ref, yacc_ref, dtv_ref, st_ref):
    tp = zx_ref.shape[1]
    n_chunks = tp // CHUNK
    gn = SSD_GROUPS * SSD_STATE

    valid = lax.broadcasted_iota(jnp.int32, (tp, LANES), 0) >= PAD
    dsum = dsk_ref[0:1, :] + dsk_ref[1:2, :]
    for c0 in range(0, SSD_CONV_DIM, LANES):
        x = jnp.where(valid, zx_ref[0, :, D_SSD + c0:D_SSD + c0 + LANES].astype(F32), 0.0)
        acc = cb_ref[:, c0:c0 + LANES] + x * cw_ref[SSD_CONV // 2:SSD_CONV // 2 + 1, c0:c0 + LANES]
        for k in range(SSD_CONV):
            off = k - SSD_CONV // 2
            if off != 0:
                acc = acc + pltpu.roll(x, (-off) % tp, axis=0) * cw_ref[k:k + 1, c0:c0 + LANES]
        act = _silu(acc)
        xact_ref[:, c0:c0 + LANES] = act.astype(BF16)
        if c0 < D_SSD:
            yacc_ref[:, c0:c0 + LANES] = act * dsum[:, c0:c0 + LANES]
    dtv_ref[...] = jnp.where(valid, _softplus(dt_ref[0] + dtb_ref[...]), 0.0)
    st_ref[...] = jnp.zeros_like(st_ref)

    ii = lax.broadcasted_iota(jnp.int32, (CHUNK, CHUNK), 0)
    jj = lax.broadcasted_iota(jnp.int32, (CHUNK, CHUNK), 1)
    lower = jj <= ii
    upper = jj >= ii
    tril = jnp.where(lower, 1.0, 0.0).astype(BF16)
    triu = jnp.where(upper, 1.0, 0.0).astype(BF16)
    a_row = -jnp.exp(alog_ref[...])
    src_lane = lax.broadcasted_iota(jnp.int32, (LANES, D_SSD), 0)
    dst_head = lax.broadcasted_iota(jnp.int32, (LANES, D_SSD), 1) // SSD_HEADDIM
    half = lax.broadcasted_iota(jnp.int32, (CHUNK, LANES), 1) // SSD_HEADDIM

    def chunk_step(c, d):
        rows = pl.ds(pl.multiple_of(c * CHUNK, CHUNK), CHUNK)
        tri_col, tri_row, mask = (tril, triu, lower) if d == 0 else (triu, tril, upper)
        expand = jnp.where(src_lane == dst_head + SSD_HEADS * d, 1.0, 0.0).astype(BF16)
        dtc = dtv_ref[rows, :]
        a = dtc * a_row
        dt_exp = _dot3_right(dtc, expand)
        a_exp = dt_exp * (-jnp.exp(aexp_ref[d:d + 1, :]))
        col = _dot3_left(tri_col, a_exp)
        rowv = _dot3_right(a.T, tri_row)
        tot = col[CHUNK - 1:CHUNK, :] if d == 0 else col[0:1, :]
        ecol = jnp.exp(col)
        wst = jnp.exp(tot - col)
        dec = jnp.exp(tot)
        xdt = xact_ref[rows, 0:D_SSD].astype(F32) * dt_exp
        xdt_b = xdt.astype(BF16)
        xw = (xdt * wst).astype(BF16)
        for g in range(SSD_GROUPS):
            bg = xact_ref[rows, D_SSD + g * SSD_STATE:D_SSD + (g + 1) * SSD_STATE]
            cg = xact_ref[rows, D_SSD + gn + g * SSD_STATE:D_SSD + gn + (g + 1) * SSD_STATE]
            cb = _dot_nt(cg, bg)
            bt = bg.astype(F32).T.astype(BF16)
            for pp in range(2):
                p = 2 * g + pp
                lanes = slice(p * LANES, (p + 1) * LANES)
                s_prev = st_ref[d, p]
                y = _dot(cg, s_prev.astype(BF16)) * ecol[:, lanes]
                for hh in range(2):
                    h = 2 * p + hh
                    ccol = col[:, h * SSD_HEADDIM:h * SSD_HEADDIM + 1]
                    crow = rowv[h + SSD_HEADS * d:h + SSD_HEADS * d + 1, :]
                    lmat = jnp.exp(jnp.where(mask, ccol - crow, -jnp.inf))
                    rhs = jnp.where(half == hh, xdt_b[:, lanes], jnp.zeros((CHUNK, LANES), BF16))
                    y = y + _dot((cb * lmat).astype(BF16), rhs)
                yacc_ref[rows, lanes] += y
                st_ref[d, p] = s_prev * dec[:, lanes] + _dot(bt, xw[:, lanes])

    def scan_body(i, carry):
        chunk_step(i, 0)
        chunk_step(n_chunks - 1 - i, 1)
        return carry

    lax.fori_loop(0, n_chunks, scan_body, 0)

    gw = D_SSD // SSD_GROUPS

    def gate_body(c, carry):
        rows = pl.ds(pl.multiple_of(c * CHUNK, CHUNK), CHUNK)
        yz = yacc_ref[rows, :] * _silu(zx_ref[0, rows, 0:D_SSD].astype(F32))
        for g in range(SSD_GROUPS):
            seg = _rms(yz[:, g * gw:(g + 1) * gw], nw_ref[:, g * gw:(g + 1) * gw])
            o_ref[0, rows, g * gw:(g + 1) * gw] = seg.astype(o_ref.dtype)
        return carry

    lax.fori_loop(0, n_chunks, gate_body, 0)


def ssd_mixer(zx3, dt3, conv_w, conv_b, a_log, dt_bias, d_skip, norm_w):
    b, tp, wzx = zx3.shape
    lane_pad = LANES - 2 * SSD_HEADS
    dtb = jnp.pad(dt_bias.reshape(1, -1), ((0, 0), (0, lane_pad)))
    alog = jnp.pad(a_log.reshape(1, -1), ((0, 0), (0, lane_pad)))
    aexp = jnp.repeat(a_log, SSD_HEADDIM, axis=1)
    dsk = jnp.repeat(d_skip, SSD_HEADDIM, axis=1)
    return pl.pallas_call(
        _ssd_kernel, grid=(b,),
        in_specs=[pl.BlockSpec((1, tp, wzx), lambda i: (i, 0, 0)),
                  pl.BlockSpec((1, tp, LANES), lambda i: (i, 0, 0)),
                  _resident(conv_w.shape), _resident((1, SSD_CONV_DIM)),
                  _resident((1, LANES)), _resident((1, LANES)),
                  _resident((2, D_SSD)), _resident((2, D_SSD)), _resident((1, D_SSD))],
        out_specs=pl.BlockSpec((1, tp, D_SSD), lambda i: (i, 0, 0)),
        out_shape=jax.ShapeDtypeStruct((b, tp, D_SSD), BF16),
        scratch_shapes=[pltpu.VMEM((tp, SSD_CONV_DIM), BF16), pltpu.VMEM((tp, D_SSD), F32),
                        pltpu.VMEM((tp, LANES), F32),
                        pltpu.VMEM((2, SSD_HEADS // 2, SSD_STATE, LANES), F32)],
        compiler_params=_cparams(1), name="ssd_mixer")(
            zx3, dt3, conv_w, conv_b.reshape(1, -1), dtb, alog, aexp, dsk, norm_w.reshape(1, -1))


def _hy_conv_kernel(p0_ref, p1_ref, p2_ref, w0_ref, w1_ref, w2_ref, b0_ref, b1_ref, b2_ref,
                    x0_ref, v_ref):
    tp = p0_ref.shape[1]
    ct = p0_ref.shape[2]
    valid = lax.broadcasted_iota(jnp.int32, (tp, LANES), 0) >= PAD

    def conv(p_ref, w_ref, b_ref, c0):
        x = jnp.where(valid, p_ref[0, :, c0:c0 + LANES].astype(F32), 0.0)
        acc = b_ref[:, c0:c0 + LANES] + x * w_ref[HYENA_SHORT // 2:HYENA_SHORT // 2 + 1, c0:c0 + LANES]
        for k in range(HYENA_SHORT):
            off = k - HYENA_SHORT // 2
            if off != 0:
                acc = acc + pltpu.roll(x, (-off) % tp, axis=0) * w_ref[k:k + 1, c0:c0 + LANES]
        return acc

    for c0 in range(0, ct, LANES):
        x0_ref[0, :, c0:c0 + LANES] = conv(p0_ref, w0_ref, b0_ref, c0).astype(x0_ref.dtype)
        v = conv(p2_ref, w2_ref, b2_ref, c0) * conv(p1_ref, w1_ref, b1_ref, c0)
        v_ref[0, :, c0:c0 + LANES] = jnp.where(valid, v, 0.0).astype(v_ref.dtype)


def hyena_conv_gate(p3, conv_w, conv_b, ct=256):
    b, tp, d3 = p3.shape
    d = d3 // 3
    nct = d // ct
    conv_b = conv_b.reshape(1, d3)
    pspec = lambda k: pl.BlockSpec((1, tp, ct), lambda i, j: (i, 0, j + k * nct))
    wspec = lambda k: pl.BlockSpec((HYENA_SHORT, ct), lambda i, j: (0, j + k * nct))
    bspec = lambda k: pl.BlockSpec((1, ct), lambda i, j: (0, j + k * nct))
    ospec = pl.BlockSpec((1, tp, ct), lambda i, j: (i, 0, j))
    return pl.pallas_call(
        _hy_conv_kernel, grid=(b, nct),
        in_specs=[pspec(0), pspec(1), pspec(2), wspec(0), wspec(1), wspec(2),
                  bspec(0), bspec(1), bspec(2)],
        out_specs=[ospec, ospec],
        out_shape=[jax.ShapeDtypeStruct((b, tp, d), BF16)] * 2,
        compiler_params=_cparams(2), name="hyena_conv_gate")(
            p3, p3, p3, conv_w, conv_w, conv_w, conv_b, conv_b, conv_b)


def _dot_f32(a, b):
    return jnp.dot(a, b, precision=lax.Precision.HIGHEST, preferred_element_type=F32)


def _hy_filter_kernel(w1t_ref, w1c_ref, w1s_ref, b1_ref, w2_ref, b2_ref, w3_ref, b3_ref, fr_ref,
                      w4f_ref, w4b_ref, bands_ref, deltas_ref, hf_ref, hb_ref, *, t_len):
    tp = hf_ref.shape[0]
    row = lax.broadcasted_iota(jnp.int32, (tp, 1), 0)
    pos = row.astype(F32)
    t = pos / (t_len - 1)
    ang = (2.0 * math.pi * pos / t_len) * bands_ref[...]
    h = t * w1t_ref[...] + _dot_f32(jnp.cos(ang), w1c_ref[...]) + _dot_f32(-jnp.sin(ang), w1s_ref[...])
    h = jnp.sin(fr_ref[0:1, :] * (h + b1_ref[...]))
    h = jnp.sin(fr_ref[1:2, :] * (_dot_f32(h, w2_ref[...]) + b2_ref[...]))
    h = jnp.sin(fr_ref[2:3, :] * (_dot_f32(h, w3_ref[...]) + b3_ref[...]))
    decay = jnp.exp(-t * deltas_ref[...]) + HYENA_SHIFT
    hf = jnp.where(row < t_len, _dot_f32(h, w4f_ref[...]) * decay, 0.0)
    hb = jnp.where((row >= 1) & (row < t_len), _dot_f32(h, w4b_ref[...]) * decay, 0.0)
    norm = (jnp.sum(jnp.abs(hf), axis=0, keepdims=True)
            + jnp.sum(jnp.abs(hb), axis=0, keepdims=True))
    hf_ref[...] = hf / norm
    hb_ref[...] = hb / norm


def hyena_filters(tp, t_len, w1, b1, w2, b2, w3, b3, w4, freq, ct=256):
    d = w4.shape[1] // 2
    fw = w1.shape[1]
    nct = d // ct
    bands = jnp.linspace(1e-4, HYENA_BANDS - 1, HYENA_BANDS, dtype=F32).reshape(1, -1)
    max_decay = math.log(HYENA_TARGET) / HYENA_FAST_DECAY
    min_decay = math.log(HYENA_TARGET) / HYENA_SLOW_DECAY
    deltas = jnp.abs(jnp.linspace(min_decay, max_decay, d, dtype=F32)).reshape(1, -1)
    nb = HYENA_BANDS
    full = lambda a: pl.BlockSpec(a.shape, lambda j: (0,) * a.ndim)
    args = [w1[0:1], w1[1:1 + nb], w1[1 + nb:], b1.reshape(1, fw), w2, b2.reshape(1, fw),
            w3, b3.reshape(1, fw), freq]
    return pl.pallas_call(
        functools.partial(_hy_filter_kernel, t_len=t_len), grid=(nct,),
        in_specs=[full(a) for a in args] + [
            pl.BlockSpec((fw, ct), lambda j: (0, j)), pl.BlockSpec((fw, ct), lambda j: (0, j + nct)),
            full(bands), pl.BlockSpec((1, ct), lambda j: (0, j))],
        out_specs=[pl.BlockSpec((tp, ct), lambda j: (0, j))] * 2,
        out_shape=[jax.ShapeDtypeStruct((tp, d), F32)] * 2,
        compiler_params=_cparams(1), name="hyena_filters")(*args, w4, w4, bands, deltas)


def _dft_table_kernel(cos_ref, sin_ref, *, period):
    tr, tp = cos_ref.shape
    a = lax.broadcasted_iota(jnp.int32, (tr, tp), 0) + pl.program_id(0) * tr
    b = lax.broadcasted_iota(jnp.int32, (tr, tp), 1)
    n = (2 * a + 1) * (2 * b + 1)
    q = jnp.floor(n.astype(F32) * (1.0 / period)).astype(jnp.int32)
    r = (n - q * period).astype(F32)
    ang = r * (2.0 * math.pi / period)
    cos_ref[...] = jnp.cos(ang).astype(cos_ref.dtype)
    sin_ref[...] = jnp.sin(ang).astype(sin_ref.dtype)


def dft_tables(tp, tr=272):
    return pl.pallas_call(
        functools.partial(_dft_table_kernel, period=8 * tp), grid=(tp // tr,),
        out_specs=[pl.BlockSpec((tr, tp), lambda i: (i, 0))] * 2,
        out_shape=[jax.ShapeDtypeStruct((tp, tp), BF16)] * 2,
        compiler_params=_cparams(1), name="dft_tables")()


def _split2(x):
    hi = x.astype(BF16)
    return hi, (x - hi.astype(F32)).astype(BF16)


def _hy_spectrum_kernel(hf_ref, hb_ref, cm_ref, sm_ref, p_ref, q_ref):
    tp = hf_ref.shape[0]
    big_l = 2 * tp
    phi = (lax.broadcasted_iota(jnp.int32, (tp, 1), 0).astype(F32) + 0.5) * (math.pi / big_l)
    cphi, sphi = jnp.cos(phi), jnp.sin(phi)
    cm, sm = cm_ref[...], sm_ref[...]

    def tdot(table, x):
        hi, lo = _split2(x)
        return _dot(table, hi) + _dot(table, lo)

    hs = hf_ref[...] + hb_ref[...]
    hd = hf_ref[...] - hb_ref[...]
    scale = 2.0 / big_l
    p_ref[...] = (tdot(cm, hs) * cphi + tdot(sm, hs) * sphi) * scale
    q_ref[...] = (tdot(sm, hd) * cphi - tdot(cm, hd) * sphi) * scale


def hyena_spectrum(hf, hb, cm, sm, ct=256):
    tp, d = hf.shape
    cspec = pl.BlockSpec((tp, ct), lambda j: (0, j))
    return pl.pallas_call(
        _hy_spectrum_kernel, grid=(d // ct,),
        in_specs=[cspec, cspec, _resident((tp, tp)), _resident((tp, tp))],
        out_specs=[cspec, cspec],
        out_shape=[jax.ShapeDtypeStruct((tp, d), F32)] * 2,
        compiler_params=_cparams(1), name="hyena_spectrum")(hf, hb, cm, sm)


def _hy_longconv_kernel(v_ref, x0_ref, p_ref, q_ref, skip_ref, cm_ref, sm_ref, o_ref):
    cm, sm = cm_ref[...], sm_ref[...]
    v = v_ref[0]
    a = _dot(cm, v)
    b = _dot(sm, v)
    p, q = p_ref[...], q_ref[...]
    yc = (a * p - b * q).astype(BF16)
    ys = (a * q + b * p).astype(BF16)
    y = _dot(cm, yc) + _dot(sm, ys)
    y = (y + v.astype(F32) * skip_ref[...]) * x0_ref[0].astype(F32)
    o_ref[0] = y.astype(o_ref.dtype)


def hyena_longconv(v3, x03, p, q, skip, cm, sm, ct=256):
    b, tp, d = v3.shape
    bspec = pl.BlockSpec((1, tp, ct), lambda j, i: (i, 0, j))
    cspec = pl.BlockSpec((tp, ct), lambda j, i: (0, j))
    return pl.pallas_call(
        _hy_longconv_kernel, grid=(d // ct, b),
        in_specs=[bspec, bspec, cspec, cspec, pl.BlockSpec((1, ct), lambda j, i: (0, j)),
                  _resident((tp, tp)), _resident((tp, tp))],
        out_specs=bspec,
        out_shape=jax.ShapeDtypeStruct((b, tp, d), BF16),
        compiler_params=_cparams(2), name="hyena_longconv")(
            v3, x03, p, q, skip.reshape(1, d), cm, sm)


def _router_kernel(x_ref, g_ref, wr_ref, gate_ref):
    xn = _rms(x_ref[...], g_ref[...])
    tm = xn.shape[0]
    lane = lax.broadcasted_iota(jnp.int32, (tm, LANES), 1)
    logits = jnp.where(lane < N_EXPERTS, _dot_f32(xn, wr_ref[...]), -jnp.inf)
    v1 = jnp.max(logits, axis=-1, keepdims=True)
    i1 = jnp.min(jnp.where(logits == v1, lane, LANES), axis=-1, keepdims=True)
    rest = jnp.where(lane == i1, -jnp.inf, logits)
    v2 = jnp.max(rest, axis=-1, keepdims=True)
    i2 = jnp.min(jnp.where(rest == v2, lane, LANES), axis=-1, keepdims=True)
    e2 = jnp.exp(v2 - v1)
    den = 1.0 + e2
    gate_ref[...] = jnp.where(lane == i1, 1.0 / den, 0.0) + jnp.where(lane == i2, e2 / den, 0.0)


def route_top2(x, g, router, tm=512):
    n_rows, d = x.shape
    tm = _row_tile(n_rows, tm)
    wr = jnp.pad(router, ((0, 0), (0, LANES - router.shape[1])))
    return pl.pallas_call(
        _router_kernel, grid=(n_rows // tm,),
        in_specs=[pl.BlockSpec((tm, d), lambda i: (i, 0)), _resident((1, d)), _resident(wr.shape)],
        out_specs=pl.BlockSpec((tm, LANES), lambda i: (i, 0)),
        out_shape=jax.ShapeDtypeStruct((n_rows, LANES), F32),
        compiler_params=_cparams(1), name="moe_router")(x, g.reshape(1, d), wr)


def _moe_dense_kernel(x_ref, g_ref, gate_ref, wg_ref, wu_ref, wd_ref, o_ref, xn_ref, acc_ref):
    e = pl.program_id(1)
    f = pl.program_id(2)

    @pl.when((e == 0) & (f == 0))
    def _():
        x = x_ref[...]
        xn_ref[...] = _rms(x, g_ref[...]).astype(BF16)
        acc_ref[...] = x

    xn = xn_ref[...]
    lane = lax.broadcasted_iota(jnp.int32, gate_ref.shape, 1)
    ge = jnp.sum(jnp.where(lane == e, gate_ref[...], 0.0), axis=-1, keepdims=True)
    hm = _silu(_dot(xn, wg_ref[0])) * _dot(xn, wu_ref[0]) * ge
    acc_ref[...] += _dot(hm.astype(BF16), wd_ref[0])

    @pl.when((e == pl.num_programs(1) - 1) & (f == pl.num_programs(2) - 1))
    def _():
        o_ref[...] = acc_ref[...]


def moe_dense(x, g, gates, wg, wu, wd, tm=1024, tf=512):
    n_rows, d = x.shape
    tm = _row_tile(n_rows, tm)
    n_e, _, fe = wg.shape
    return pl.pallas_call(
        _moe_dense_kernel, grid=(n_rows // tm, n_e, fe // tf),
        in_specs=[pl.BlockSpec((tm, d), lambda i, e, f: (i, 0)), _resident((1, d)),
                  pl.BlockSpec((tm, LANES), lambda i, e, f: (i, 0)),
                  pl.BlockSpec((1, d, tf), lambda i, e, f: (e, 0, f)),
                  pl.BlockSpec((1, d, tf), lambda i, e, f: (e, 0, f)),
                  pl.BlockSpec((1, tf, d), lambda i, e, f: (e, f, 0))],
        out_specs=pl.BlockSpec((tm, d), lambda i, e, f: (i, 0)),
        out_shape=jax.ShapeDtypeStruct((n_rows, d), F32),
        scratch_shapes=[pltpu.VMEM((tm, d), BF16), pltpu.VMEM((tm, d), F32)],
        compiler_params=_cparams(3), name="moe_dense")(x, g.reshape(1, d), gates, wg, wu, wd)


def kernel(x, meta_tokens, norm_mix_even, w_in_ab, ssd_conv_w, ssd_conv_b, ssd_a_log,
           ssd_dt_bias, ssd_d, ssd_norm_w, diff_lambda, diff_subln_w, w_out_ab,
           norm_ffn_even, ffn_w_gate, ffn_w_up, ffn_w_down, norm_mix_odd, hy_w_in, hy_b_in,
           hy_conv_w, hy_conv_b, hy_f_w1, hy_f_b1, hy_f_w2, hy_f_b2, hy_f_w3, hy_f_b3,
           hy_f_w4, hy_f_freq, hy_skip, hy_w_out, hy_b_out, norm_ffn_odd, moe_router,
           moe_w_gate, moe_w_up, moe_w_down, final_norm):
    b, seq, d = x.shape
    assert d == D_MODEL and seq % 512 == 0
    tp = seq + CHUNK
    t_len = seq + N_META
    n = b * tp
    meta = jnp.broadcast_to(meta_tokens[None].astype(x.dtype), (b, N_META, d))
    h = jnp.concatenate([jnp.zeros((b, PAD, d), x.dtype), meta, x], axis=1).reshape(n, d)
    cos, sin = rope_tables(tp)
    depth = norm_mix_even.shape[0] + norm_mix_odd.shape[0]
    o1 = D_SSD + SSD_CONV_DIM
    o2 = o1 + 2 * SSD_HEADS
    for layer in range(depth):
        i = layer // 2
        if layer % 2 == 0:
            w = w_in_ab[i]
            w_dt = jnp.pad(w[:, o1:o2], ((0, 0), (0, LANES - 2 * SSD_HEADS)))
            zx, dt, qkv = rms_proj(
                h, norm_mix_even[i],
                [w[:, :o1].astype(BF16), w_dt.astype(BF16), w[:, o2:].astype(BF16)],
                None, [BF16, F32, BF16])
            ssd_out = ssd_mixer(zx.reshape(b, tp, -1), dt.reshape(b, tp, -1), ssd_conv_w[i],
                                ssd_conv_b[i], ssd_a_log[i], ssd_dt_bias[i], ssd_d[i],
                                ssd_norm_w[i])
            lambda_init = 0.8 - 0.6 * math.exp(-0.3 * layer)
            attn_out = diff_attention(qkv.reshape(b, tp, -1), cos, sin, diff_lambda[i],
                                      diff_subln_w[i], lambda_init)
            wo = w_out_ab[i].astype(BF16)
            h = proj_residual(h, [ssd_out.reshape(n, -1), attn_out.reshape(n, -1)],
                              [wo[:D_SSD], wo[D_SSD:]], None)
            h = ffn_residual(h, norm_ffn_even[i], ffn_w_gate[i].astype(BF16),
                             ffn_w_up[i].astype(BF16), ffn_w_down[i].astype(BF16))
        else:
            (p,) = rms_proj(h, norm_mix_odd[i], [hy_w_in[i].astype(BF16)],
                            [hy_b_in[i].reshape(1, -1)], [BF16])
            x0, v = hyena_conv_gate(p.reshape(b, tp, -1), hy_conv_w[i], hy_conv_b[i])
            hf, hb = hyena_filters(tp, t_len, hy_f_w1[i], hy_f_b1[i], hy_f_w2[i], hy_f_b2[i],
                                   hy_f_w3[i], hy_f_b3[i], hy_f_w4[i], hy_f_freq[i])
            cm, sm = dft_tables(tp)
            fp, fq = hyena_spectrum(hf, hb, cm, sm)
            y = hyena_longconv(v, x0, fp, fq, hy_skip[i], cm, sm)
            h = proj_residual(h, [y.reshape(n, -1)], [hy_w_out[i].astype(BF16)], hy_b_out[i])
            gates = route_top2(h, norm_ffn_odd[i], moe_router[i])
            h = moe_dense(h, norm_ffn_odd[i], gates, moe_w_gate[i].astype(BF16),
                          moe_w_up[i].astype(BF16), moe_w_down[i].astype(BF16))
    return final_rmsnorm(h.reshape(b, tp, d), final_norm, seq)
```

```python
import functools
import math

import jax
import jax.numpy as jnp
from jax import lax
from jax.experimental import pallas as pl
from jax.experimental.pallas import tpu as pltpu

F32 = jnp.float32
BF16 = jnp.bfloat16

D_MODEL = 1024
N_META = 16
EPS = 1e-5
D_SSD = 512
SSD_HEADDIM = 64
SSD_HEADS = 8
SSD_GROUPS = 2
SSD_STATE = 128
SSD_CONV = 5
CHUNK = 128
SSD_CONV_DIM = D_SSD + 2 * SSD_GROUPS * SSD_STATE
D_DIFF = 512
DIFF_HEADDIM = 64
DIFF_HEADS = 4
ROPE_THETA = 500000.0
ROPE_DIM = 16
HYENA_SHORT = 3
HYENA_BANDS = 16
HYENA_SHIFT = 0.05
HYENA_TARGET = 1e-2
HYENA_FAST_DECAY = 0.3
HYENA_SLOW_DECAY = 1.5
N_EXPERTS = 8
TOP_K = 2

LANES = 128
PAD = CHUNK - N_META
VMEM_LIMIT = 48 * 1024 * 1024
MOE_TILE = 512


def _cparams(n_axes):
    return pltpu.CompilerParams(dimension_semantics=("arbitrary",) * n_axes,
                                vmem_limit_bytes=VMEM_LIMIT)


def _resident(shape):
    zeros = (0,) * len(shape)
    return pl.BlockSpec(shape, lambda *_: zeros, pipeline_mode=pl.Buffered(1))


def _row_tile(n_rows, preferred):
    assert n_rows % CHUNK == 0
    tm = preferred
    while n_rows % tm:
        tm //= 2
    return tm


def _dot(a, b):
    return jnp.dot(a, b, preferred_element_type=F32)


def _dot_nt(a, b):
    return lax.dot_general(a, b, (((1,), (1,)), ((), ())), preferred_element_type=F32)


def _split3(x):
    hi = x.astype(BF16)
    r = x - hi.astype(F32)
    mid = r.astype(BF16)
    lo = (r - mid.astype(F32)).astype(BF16)
    return hi, mid, lo


def _dot3_right(x, m):
    hi, mid, lo = _split3(x)
    return _dot(hi, m) + _dot(mid, m) + _dot(lo, m)


def _dot3_left(m, x):
    hi, mid, lo = _split3(x)
    return _dot(m, hi) + _dot(m, mid) + _dot(m, lo)


def _rms(x, w):
    return x * lax.rsqrt(jnp.mean(x * x, axis=-1, keepdims=True) + EPS) * w


def _silu(x):
    return x * jax.nn.sigmoid(x)


def _rms_proj_kernel(*refs, n_out, has_bias, col_chunk):
    x_ref, g_ref = refs[:2]
    w_refs = refs[2:2 + n_out]
    b_refs = refs[2 + n_out:2 + 2 * n_out] if has_bias else (None,) * n_out
    o_refs = refs[-n_out:]
    xn = _rms(x_ref[...], g_ref[...]).astype(BF16)
    for w_ref, b_ref, o_ref in zip(w_refs, b_refs, o_refs):
        n = o_ref.shape[-1]
        for c0 in range(0, n, col_chunk):
            c1 = min(n, c0 + col_chunk)
            y = _dot(xn, w_ref[:, c0:c1])
            if b_ref is not None:
                y = y + b_ref[:, c0:c1]
            o_ref[:, c0:c1] = y.astype(o_ref.dtype)


def rms_proj(x, g, ws, bs, out_dtypes, tm=512):
    n_rows, d = x.shape
    tm = _row_tile(n_rows, tm)
    n_out = len(ws)
    has_bias = bs is not None
    in_specs = [pl.BlockSpec((tm, d), lambda i: (i, 0)), _resident((1, d))]
    in_specs += [_resident(w.shape) for w in ws]
    args = [x, g.reshape(1, d)] + list(ws)
    if has_bias:
        in_specs += [_resident(b.shape) for b in bs]
        args += list(bs)
    out_shape = [jax.ShapeDtypeStruct((n_rows, w.shape[1]), dt) for w, dt in zip(ws, out_dtypes)]
    out_specs = [pl.BlockSpec((tm, w.shape[1]), lambda i: (i, 0)) for w in ws]
    return pl.pallas_call(
        functools.partial(_rms_proj_kernel, n_out=n_out, has_bias=has_bias, col_chunk=512),
        grid=(n_rows // tm,), in_specs=in_specs, out_specs=out_specs, out_shape=out_shape,
        compiler_params=_cparams(1), name="rms_proj")(*args)


def _proj_res_kernel(*refs, n_in, has_bias):
    res_ref = refs[0]
    a_refs = refs[1:1 + n_in]
    w_refs = refs[1 + n_in:1 + 2 * n_in]
    b_ref = refs[1 + 2 * n_in] if has_bias else None
    o_ref = refs[-1]
    acc = res_ref[...]
    for a_ref, w_ref in zip(a_refs, w_refs):
        acc = acc + _dot(a_ref[...], w_ref[...])
    if b_ref is not None:
        acc = acc + b_ref[...]
    o_ref[...] = acc


def proj_residual(res, a_list, w_list, bias, tm=512):
    n_rows, d = res.shape
    tm = _row_tile(n_rows, tm)
    n_in = len(a_list)
    has_bias = bias is not None
    in_specs = [pl.BlockSpec((tm, d), lambda i: (i, 0))]
    in_specs += [pl.BlockSpec((tm, a.shape[1]), lambda i: (i, 0)) for a in a_list]
    in_specs += [_resident(w.shape) for w in w_list]
    args = [res] + list(a_list) + list(w_list)
    if has_bias:
        in_specs.append(_resident((1, d)))
        args.append(bias.reshape(1, d))
    return pl.pallas_call(
        functools.partial(_proj_res_kernel, n_in=n_in, has_bias=has_bias),
        grid=(n_rows // tm,), in_specs=in_specs,
        out_specs=pl.BlockSpec((tm, d), lambda i: (i, 0)),
        out_shape=jax.ShapeDtypeStruct((n_rows, d), F32),
        compiler_params=_cparams(1), name="proj_residual")(*args)


def _ffn_kernel(x_ref, g_ref, wg_ref, wu_ref, wd_ref, o_ref, hm_ref, *, tf):
    x = x_ref[...]
    xn = _rms(x, g_ref[...]).astype(BF16)
    f = wg_ref.shape[1]
    for c0 in range(0, f, tf):
        gt = _dot(xn, wg_ref[:, c0:c0 + tf])
        ut = _dot(xn, wu_ref[:, c0:c0 + tf])
        hm_ref[:, c0:c0 + tf] = (_silu(gt) * ut).astype(BF16)
    o_ref[...] = x + _dot(hm_ref[...], wd_ref[...])


def ffn_residual(x, g, wg, wu, wd, tm=512, tf=256):
    n_rows, d = x.shape
    tm = _row_tile(n_rows, tm)
    f = wg.shape[1]
    return pl.pallas_call(
        functools.partial(_ffn_kernel, tf=tf),
        grid=(n_rows // tm,),
        in_specs=[pl.BlockSpec((tm, d), lambda i: (i, 0)), _resident((1, d)),
                  _resident(wg.shape), _resident(wu.shape), _resident(wd.shape)],
        out_specs=pl.BlockSpec((tm, d), lambda i: (i, 0)),
        out_shape=jax.ShapeDtypeStruct((n_rows, d), F32),
        scratch_shapes=[pltpu.VMEM((tm, f), BF16)],
        compiler_params=_cparams(1), name="ffn")(x, g.reshape(1, d), wg, wu, wd)


def _final_norm_kernel(x_ref, g_ref, o_ref):
    seq = o_ref.shape[1]
    for r0 in range(0, seq, 512):
        o_ref[0, r0:r0 + 512, :] = _rms(x_ref[0, CHUNK + r0:CHUNK + r0 + 512, :], g_ref[...])


def final_rmsnorm(h3, g, seq):
    b, tp, d = h3.shape
    assert seq % 512 == 0 and tp == seq + CHUNK
    return pl.pallas_call(
        _final_norm_kernel, grid=(b,),
        in_specs=[pl.BlockSpec((1, tp, d), lambda i: (i, 0, 0)), _resident((1, d))],
        out_specs=pl.BlockSpec((1, seq, d), lambda i: (i, 0, 0)),
        out_shape=jax.ShapeDtypeStruct((b, seq, d), F32),
        compiler_params=_cparams(1), name="final_norm")(h3, g.reshape(1, d))


def _rope_table_kernel(freq_ref, cos_ref, sin_ref):
    tp = cos_ref.shape[0]
    pos = (lax.broadcasted_iota(jnp.int32, (tp, LANES), 0) - PAD).astype(F32)
    lane = lax.broadcasted_iota(jnp.int32, (tp, LANES), 1) % DIFF_HEADDIM
    ang = pos * freq_ref[...]
    half = ROPE_DIM // 2
    rot = lane < ROPE_DIM
    cos_ref[...] = jnp.where(rot, jnp.cos(ang), 1.0)
    sin_ref[...] = jnp.where(lane < half, -jnp.sin(ang), jnp.where(rot, jnp.sin(ang), 0.0))


def rope_tables(tp):
    inv_freq = ROPE_THETA ** (-jnp.arange(0, ROPE_DIM, 2, dtype=F32) / ROPE_DIM)
    per_sub = jnp.concatenate([inv_freq, inv_freq, jnp.zeros((DIFF_HEADDIM - ROPE_DIM,), F32)])
    freq = jnp.concatenate([per_sub, per_sub]).reshape(1, LANES)
    return pl.pallas_call(
        _rope_table_kernel, grid=(1,),
        in_specs=[_resident((1, LANES))],
        out_specs=[pl.BlockSpec((tp, LANES), lambda i: (0, 0))] * 2,
        out_shape=[jax.ShapeDtypeStruct((tp, LANES), F32)] * 2,
        compiler_params=_cparams(1), name="rope_tables")(freq)


def _swap_matrix():
    src = lax.broadcasted_iota(jnp.int32, (LANES, LANES), 0)
    dst = lax.broadcasted_iota(jnp.int32, (LANES, LANES), 1)
    sub = dst % DIFF_HEADDIM
    half = ROPE_DIM // 2
    partner = jnp.where(sub < half, dst + half, jnp.where(sub < ROPE_DIM, dst - half, dst))
    return jnp.where(src == partner, 1.0, 0.0).astype(BF16)


def _rotate(x_bf16, cos, sin_signed, swap):
    return x_bf16.astype(F32) * cos + _dot(x_bf16, swap) * sin_signed


def _diff_attn_kernel(q_ref, k_ref, v_ref, cos_ref, sin_ref, lam_ref, w_ref, o_ref, kr_ref,
                      *, lambda_init, tq, unroll):
    tp = k_ref.shape[1]
    swap = _swap_matrix()
    kr_ref[...] = _rotate(k_ref[0], cos_ref[...], sin_ref[...], swap).astype(BF16)
    scale = DIFF_HEADDIM ** -0.5 * math.log2(math.e)
    sub_head = lax.broadcasted_iota(jnp.int32, (tq, LANES), 1) // DIFF_HEADDIM
    real_key = lax.broadcasted_iota(jnp.int32, (tq, CHUNK), 1) >= PAD
    lp = lam_ref[...]
    lam = (jnp.exp(jnp.sum(lp[0:1] * lp[1:2], axis=-1, keepdims=True))
           - jnp.exp(jnp.sum(lp[2:3] * lp[3:4], axis=-1, keepdims=True)) + lambda_init)

    def q_block(qi, carry):
        rows = pl.ds(pl.multiple_of(qi * tq, 16), tq)
        qr = (_rotate(q_ref[0, rows, :], cos_ref[rows, :], sin_ref[rows, :], swap) * scale).astype(BF16)
        maps = []
        for m in range(2):
            qm = jnp.where(sub_head == m, qr, jnp.zeros_like(qr))
            sa = jnp.where(real_key, _dot_nt(qm, kr_ref[0:CHUNK, :]), -jnp.inf)
            sb = _dot_nt(qm, kr_ref[CHUNK:, :])
            mx = jnp.maximum(jnp.max(sa, axis=-1, keepdims=True), jnp.max(sb, axis=-1, keepdims=True))
            pa = jnp.exp2(sa - mx)
            pb = jnp.exp2(sb - mx)
            den = jnp.sum(pa, axis=-1, keepdims=True) + jnp.sum(pb, axis=-1, keepdims=True)
            pv = _dot(pa.astype(BF16), v_ref[0, 0:CHUNK, :]) + _dot(pb.astype(BF16), v_ref[0, CHUNK:, :])
            maps.append(pv * (1.0 / den))
        o = maps[0] - lam * maps[1]
        o_ref[0, rows, :] = (_rms(o, w_ref[...]) * (1.0 - lambda_init)).astype(o_ref.dtype)
        return carry

    lax.fori_loop(0, tp // tq, q_block, 0, unroll=unroll)


def diff_attention(qkv3, cos, sin, lam_params, subln_w, lambda_init, tq=272, unroll=2):
    b, tp, _ = qkv3.shape
    hw = 2 * DIFF_HEADDIM
    assert hw == LANES and tp % tq == 0 and tq % 16 == 0
    head = lambda k: pl.BlockSpec((1, tp, hw), lambda i, h: (i, 0, k * DIFF_HEADS + h))
    return pl.pallas_call(
        functools.partial(_diff_attn_kernel, lambda_init=lambda_init, tq=tq, unroll=unroll),
        grid=(b, DIFF_HEADS),
        in_specs=[head(0), head(1), head(2), _resident((tp, LANES)), _resident((tp, LANES)),
                  _resident(lam_params.shape), _resident((1, hw))],
        out_specs=head(0),
        out_shape=jax.ShapeDtypeStruct((b, tp, D_DIFF), BF16),
        scratch_shapes=[pltpu.VMEM((tp, hw), BF16)],
        compiler_params=_cparams(2), name="diff_attention")(
            qkv3, qkv3, qkv3, cos, sin, lam_params, subln_w.reshape(1, hw))


def _softplus(x):
    return jnp.maximum(x, 0.0) + jnp.log1p(jnp.exp(-jnp.abs(x)))


def _ssd_kernel(zx_ref, dt_ref, cw_ref, cb_ref, dtb_ref, alog_ref, aexp_ref, dsk_ref, nw_ref,
                o_ref, xact_ref, yacc_ref, dtv_ref, st_ref):
    tp = zx_ref.shape[1]
    n_chunks = tp // CHUNK
    gn = SSD_GROUPS * SSD_STATE

    valid = lax.broadcasted_iota(jnp.int32, (tp, LANES), 0) >= PAD
    dsum = dsk_ref[0:1, :] + dsk_ref[1:2, :]
    for c0 in range(0, SSD_CONV_DIM, LANES):
        x = jnp.where(valid, zx_ref[0, :, D_SSD + c0:D_SSD + c0 + LANES].astype(F32), 0.0)
        acc = cb_ref[:, c0:c0 + LANES] + x * cw_ref[SSD_CONV // 2:SSD_CONV // 2 + 1, c0:c0 + LANES]
        for k in range(SSD_CONV):
            off = k - SSD_CONV // 2
            if off != 0:
                acc = acc + pltpu.roll(x, (-off) % tp, axis=0) * cw_ref[k:k + 1, c0:c0 + LANES]
        act = _silu(acc)
        xact_ref[:, c0:c0 + LANES] = act.astype(BF16)
        if c0 < D_SSD:
            yacc_ref[:, c0:c0 + LANES] = act * dsum[:, c0:c0 + LANES]
    dtv_ref[...] = jnp.where(valid, _softplus(dt_ref[0] + dtb_ref[...]), 0.0)
    st_ref[...] = jnp.zeros_like(st_ref)

    ii = lax.broadcasted_iota(jnp.int32, (CHUNK, CHUNK), 0)
    jj = lax.broadcasted_iota(jnp.int32, (CHUNK, CHUNK), 1)
    lower = jj <= ii
    upper = jj >= ii
    tril = jnp.where(lower, 1.0, 0.0).astype(BF16)
    triu = jnp.where(upper, 1.0, 0.0).astype(BF16)
    a_row = -jnp.exp(alog_ref[...])
    src_lane = lax.broadcasted_iota(jnp.int32, (LANES, D_SSD), 0)
    dst_head = lax.broadcasted_iota(jnp.int32, (LANES, D_SSD), 1) // SSD_HEADDIM
    half = lax.broadcasted_iota(jnp.int32, (CHUNK, LANES), 1) // SSD_HEADDIM

    def chunk_step(c, d):
        rows = pl.ds(pl.multiple_of(c * CHUNK, CHUNK), CHUNK)
        tri_col, tri_row, mask = (tril, triu, lower) if d == 0 else (triu, tril, upper)
        expand = jnp.where(src_lane == dst_head + SSD_HEADS * d, 1.0, 0.0).astype(BF16)
        dtc = dtv_ref[rows, :]
        a = dtc * a_row
        dt_exp = _dot3_right(dtc, expand)
        a_exp = dt_exp * (-jnp.exp(aexp_ref[d:d + 1, :]))
        col = _dot3_left(tri_col, a_exp)
        rowv = _dot3_right(a.T, tri_row)
        tot = col[CHUNK - 1:CHUNK, :] if d == 0 else col[0:1, :]
        ecol = jnp.exp(col)
        wst = jnp.exp(tot - col)
        dec = jnp.exp(tot)
        xdt = xact_ref[rows, 0:D_SSD].astype(F32) * dt_exp
        xdt_b = xdt.astype(BF16)
        xw = (xdt * wst).astype(BF16)
        for g in range(SSD_GROUPS):
            bg = xact_ref[rows, D_SSD + g * SSD_STATE:D_SSD + (g + 1) * SSD_STATE]
            cg = xact_ref[rows, D_SSD + gn + g * SSD_STATE:D_SSD + gn + (g + 1) * SSD_STATE]
            cb = _dot_nt(cg, bg)
            bt = bg.astype(F32).T.astype(BF16)
            for pp in range(2):
                p = 2 * g + pp
                lanes = slice(p * LANES, (p + 1) * LANES)
                s_prev = st_ref[d, p]
                y = _dot(cg, s_prev.astype(BF16)) * ecol[:, lanes]
                for hh in range(2):
                    h = 2 * p + hh
                    ccol = col[:, h * SSD_HEADDIM:h * SSD_HEADDIM + 1]
                    crow = rowv[h + SSD_HEADS * d:h + SSD_HEADS * d + 1, :]
                    lmat = jnp.exp(jnp.where(mask, ccol - crow, -jnp.inf))
                    rhs = jnp.where(half == hh, xdt_b[:, lanes], jnp.zeros((CHUNK, LANES), BF16))
                    y = y + _dot((cb * lmat).astype(BF16), rhs)
                yacc_ref[rows, lanes] += y
                st_ref[d, p] = s_prev * dec[:, lanes] + _dot(bt, xw[:, lanes])

    def scan_body(i, carry):
        chunk_step(i, 0)
        chunk_step(n_chunks - 1 - i, 1)
        return carry

    lax.fori_loop(0, n_chunks, scan_body, 0)

    gw = D_SSD // SSD_GROUPS

    def gate_body(c, carry):
        rows = pl.ds(pl.multiple_of(c * CHUNK, CHUNK), CHUNK)
        yz = yacc_ref[rows, :] * _silu(zx_ref[0, rows, 0:D_SSD].astype(F32))
        for g in range(SSD_GROUPS):
            seg = _rms(yz[:, g * gw:(g + 1) * gw], nw_ref[:, g * gw:(g + 1) * gw])
            o_ref[0, rows, g * gw:(g + 1) * gw] = seg.astype(o_ref.dtype)
        return carry

    lax.fori_loop(0, n_chunks, gate_body, 0)


def ssd_mixer(zx3, dt3, conv_w, conv_b, a_log, dt_bias, d_skip, norm_w):
    b, tp, wzx = zx3.shape
    lane_pad = LANES - 2 * SSD_HEADS
    dtb = jnp.pad(dt_bias.reshape(1, -1), ((0, 0), (0, lane_pad)))
    alog = jnp.pad(a_log.reshape(1, -1), ((0, 0), (0, lane_pad)))
    aexp = jnp.repeat(a_log, SSD_HEADDIM, axis=1)
    dsk = jnp.repeat(d_skip, SSD_HEADDIM, axis=1)
    return pl.pallas_call(
        _ssd_kernel, grid=(b,),
        in_specs=[pl.BlockSpec((1, tp, wzx), lambda i: (i, 0, 0)),
                  pl.BlockSpec((1, tp, LANES), lambda i: (i, 0, 0)),
                  _resident(conv_w.shape), _resident((1, SSD_CONV_DIM)),
                  _resident((1, LANES)), _resident((1, LANES)),
                  _resident((2, D_SSD)), _resident((2, D_SSD)), _resident((1, D_SSD))],
        out_specs=pl.BlockSpec((1, tp, D_SSD), lambda i: (i, 0, 0)),
        out_shape=jax.ShapeDtypeStruct((b, tp, D_SSD), BF16),
        scratch_shapes=[pltpu.VMEM((tp, SSD_CONV_DIM), BF16), pltpu.VMEM((tp, D_SSD), F32),
                        pltpu.VMEM((tp, LANES), F32),
                        pltpu.VMEM((2, SSD_HEADS // 2, SSD_STATE, LANES), F32)],
        compiler_params=_cparams(1), name="ssd_mixer")(
            zx3, dt3, conv_w, conv_b.reshape(1, -1), dtb, alog, aexp, dsk, norm_w.reshape(1, -1))


def _hy_conv_kernel(p0_ref, p1_ref, p2_ref, w0_ref, w1_ref, w2_ref, b0_ref, b1_ref, b2_ref,
                    x0_ref, v_ref):
    tp = p0_ref.shape[1]
    ct = p0_ref.shape[2]
    valid = lax.broadcasted_iota(jnp.int32, (tp, LANES), 0) >= PAD

    def conv(p_ref, w_ref, b_ref, c0):
        x = jnp.where(valid, p_ref[0, :, c0:c0 + LANES].astype(F32), 0.0)
        acc = b_ref[:, c0:c0 + LANES] + x * w_ref[HYENA_SHORT // 2:HYENA_SHORT // 2 + 1, c0:c0 + LANES]
        for k in range(HYENA_SHORT):
            off = k - HYENA_SHORT // 2
            if off != 0:
                acc = acc + pltpu.roll(x, (-off) % tp, axis=0) * w_ref[k:k + 1, c0:c0 + LANES]
        return acc

    for c0 in range(0, ct, LANES):
        x0_ref[0, :, c0:c0 + LANES] = conv(p0_ref, w0_ref, b0_ref, c0).astype(x0_ref.dtype)
        v = conv(p2_ref, w2_ref, b2_ref, c0) * conv(p1_ref, w1_ref, b1_ref, c0)
        v_ref[0, :, c0:c0 + LANES] = jnp.where(valid, v, 0.0).astype(v_ref.dtype)


def hyena_conv_gate(p3, conv_w, conv_b, ct=256):
    b, tp, d3 = p3.shape
    d = d3 // 3
    nct = d // ct
    conv_b = conv_b.reshape(1, d3)
    pspec = lambda k: pl.BlockSpec((1, tp, ct), lambda i, j: (i, 0, j + k * nct))
    wspec = lambda k: pl.BlockSpec((HYENA_SHORT, ct), lambda i, j: (0, j + k * nct))
    bspec = lambda k: pl.BlockSpec((1, ct), lambda i, j: (0, j + k * nct))
    ospec = pl.BlockSpec((1, tp, ct), lambda i, j: (i, 0, j))
    return pl.pallas_call(
        _hy_conv_kernel, grid=(b, nct),
        in_specs=[pspec(0), pspec(1), pspec(2), wspec(0), wspec(1), wspec(2),
                  bspec(0), bspec(1), bspec(2)],
        out_specs=[ospec, ospec],
        out_shape=[jax.ShapeDtypeStruct((b, tp, d), BF16)] * 2,
        compiler_params=_cparams(2), name="hyena_conv_gate")(
            p3, p3, p3, conv_w, conv_w, conv_w, conv_b, conv_b, conv_b)


def _dot_f32(a, b):
    return jnp.dot(a, b, precision=lax.Precision.HIGHEST, preferred_element_type=F32)


def _hy_filter_kernel(w1t_ref, w1c_ref, w1s_ref, b1_ref, w2_ref, b2_ref, w3_ref, b3_ref, fr_ref,
                      w4f_ref, w4b_ref, bands_ref, deltas_ref, hf_ref, hb_ref, *, t_len):
    tp = hf_ref.shape[0]
    row = lax.broadcasted_iota(jnp.int32, (tp, 1), 0)
    pos = row.astype(F32)
    t = pos / (t_len - 1)
    ang = (2.0 * math.pi * pos / t_len) * bands_ref[...]
    h = t * w1t_ref[...] + _dot_f32(jnp.cos(ang), w1c_ref[...]) + _dot_f32(-jnp.sin(ang), w1s_ref[...])
    h = jnp.sin(fr_ref[0:1, :] * (h + b1_ref[...]))
    h = jnp.sin(fr_ref[1:2, :] * (_dot_f32(h, w2_ref[...]) + b2_ref[...]))
    h = jnp.sin(fr_ref[2:3, :] * (_dot_f32(h, w3_ref[...]) + b3_ref[...]))
    decay = jnp.exp(-t * deltas_ref[...]) + HYENA_SHIFT
    hf = jnp.where(row < t_len, _dot_f32(h, w4f_ref[...]) * decay, 0.0)
    hb = jnp.where((row >= 1) & (row < t_len), _dot_f32(h, w4b_ref[...]) * decay, 0.0)
    norm = (jnp.sum(jnp.abs(hf), axis=0, keepdims=True)
            + jnp.sum(jnp.abs(hb), axis=0, keepdims=True))
    hf_ref[...] = hf / norm
    hb_ref[...] = hb / norm


def hyena_filters(tp, t_len, w1, b1, w2, b2, w3, b3, w4, freq, ct=256):
    d = w4.shape[1] // 2
    fw = w1.shape[1]
    nct = d // ct
    bands = jnp.linspace(1e-4, HYENA_BANDS - 1, HYENA_BANDS, dtype=F32).reshape(1, -1)
    max_decay = math.log(HYENA_TARGET) / HYENA_FAST_DECAY
    min_decay = math.log(HYENA_TARGET) / HYENA_SLOW_DECAY
    deltas = jnp.abs(jnp.linspace(min_decay, max_decay, d, dtype=F32)).reshape(1, -1)
    nb = HYENA_BANDS
    full = lambda a: pl.BlockSpec(a.shape, lambda j: (0,) * a.ndim)
    args = [w1[0:1], w1[1:1 + nb], w1[1 + nb:], b1.reshape(1, fw), w2, b2.reshape(1, fw),
            w3, b3.reshape(1, fw), freq]
    return pl.pallas_call(
        functools.partial(_hy_filter_kernel, t_len=t_len), grid=(nct,),
        in_specs=[full(a) for a in args] + [
            pl.BlockSpec((fw, ct), lambda j: (0, j)), pl.BlockSpec((fw, ct), lambda j: (0, j + nct)),
            full(bands), pl.BlockSpec((1, ct), lambda j: (0, j))],
        out_specs=[pl.BlockSpec((tp, ct), lambda j: (0, j))] * 2,
        out_shape=[jax.ShapeDtypeStruct((tp, d), F32)] * 2,
        compiler_params=_cparams(1), name="hyena_filters")(*args, w4, w4, bands, deltas)


def _dft_table_kernel(cs_ref, *, period):
    tr = cs_ref.shape[0]
    tp = cs_ref.shape[1] // 2
    a = lax.broadcasted_iota(jnp.int32, (tr, tp), 0) + pl.program_id(0) * tr
    b = lax.broadcasted_iota(jnp.int32, (tr, tp), 1)
    n = (2 * a + 1) * (2 * b + 1)
    q = jnp.floor(n.astype(F32) * (1.0 / period)).astype(jnp.int32)
    r = (n - q * period).astype(F32)
    ang = r * (2.0 * math.pi / period)
    cs_ref[:, :tp] = jnp.cos(ang).astype(cs_ref.dtype)
    cs_ref[:, tp:] = jnp.sin(ang).astype(cs_ref.dtype)


def dft_tables(tp, tr=272):
    return pl.pallas_call(
        functools.partial(_dft_table_kernel, period=8 * tp), grid=(tp // tr,),
        out_specs=pl.BlockSpec((tr, 2 * tp), lambda i: (i, 0)),
        out_shape=jax.ShapeDtypeStruct((tp, 2 * tp), BF16),
        compiler_params=_cparams(1), name="dft_tables")()


def _split2(x):
    hi = x.astype(BF16)
    return hi, (x - hi.astype(F32)).astype(BF16)


def _hy_spectrum_kernel(hf_ref, hb_ref, cs_ref, p_ref, q_ref, *, mc):
    tp = hf_ref.shape[0]
    big_l = 2 * tp
    scale = 2.0 / big_l
    hs_hi, hs_lo = _split2(hf_ref[...] + hb_ref[...])
    hd_hi, hd_lo = _split2(hf_ref[...] - hb_ref[...])
    for m0 in range(0, tp, mc):
        rows = slice(m0, m0 + mc)
        phi = ((lax.broadcasted_iota(jnp.int32, (mc, 1), 0) + m0).astype(F32) + 0.5) * (math.pi / big_l)
        cphi, sphi = jnp.cos(phi), jnp.sin(phi)
        cm, sm = cs_ref[rows, :tp], cs_ref[rows, tp:]
        c_hs = _dot(cm, hs_hi) + _dot(cm, hs_lo)
        s_hs = _dot(sm, hs_hi) + _dot(sm, hs_lo)
        c_hd = _dot(cm, hd_hi) + _dot(cm, hd_lo)
        s_hd = _dot(sm, hd_hi) + _dot(sm, hd_lo)
        p_ref[rows, :] = (c_hs * cphi + s_hs * sphi) * scale
        q_ref[rows, :] = (s_hd * cphi - c_hd * sphi) * scale


def hyena_spectrum(hf, hb, cs, ct=256, mc=272):
    tp, d = hf.shape
    cspec = pl.BlockSpec((tp, ct), lambda j: (0, j))
    return pl.pallas_call(
        functools.partial(_hy_spectrum_kernel, mc=mc), grid=(d // ct,),
        in_specs=[cspec, cspec, _resident(cs.shape)],
        out_specs=[cspec, cspec],
        out_shape=[jax.ShapeDtypeStruct((tp, d), F32)] * 2,
        compiler_params=_cparams(1), name="hyena_spectrum")(hf, hb, cs)


def _hy_longconv_kernel(v_ref, x0_ref, p_ref, q_ref, skip_ref, cs_ref, o_ref, ycs_ref, *, mc):
    tp = v_ref.shape[1]
    v = v_ref[0]
    for m0 in range(0, tp, mc):
        rows = slice(m0, m0 + mc)
        a = _dot(cs_ref[rows, :tp], v)
        b = _dot(cs_ref[rows, tp:], v)
        p, q = p_ref[rows, :], q_ref[rows, :]
        ycs_ref[m0:m0 + mc, :] = (a * p - b * q).astype(BF16)
        ycs_ref[tp + m0:tp + m0 + mc, :] = (a * q + b * p).astype(BF16)
    for m0 in range(0, tp, mc):
        rows = slice(m0, m0 + mc)
        y = _dot(cs_ref[rows, :], ycs_ref[...])
        y = (y + v_ref[0, rows, :].astype(F32) * skip_ref[...]) * x0_ref[0, rows, :].astype(F32)
        o_ref[0, rows, :] = y.astype(o_ref.dtype)


def hyena_longconv(v3, x03, p, q, skip, cs, ct=256, mc=272):
    b, tp, d = v3.shape
    bspec = pl.BlockSpec((1, tp, ct), lambda j, i: (i, 0, j))
    cspec = pl.BlockSpec((tp, ct), lambda j, i: (0, j))
    return pl.pallas_call(
        functools.partial(_hy_longconv_kernel, mc=mc), grid=(d // ct, b),
        in_specs=[bspec, bspec, cspec, cspec, pl.BlockSpec((1, ct), lambda j, i: (0, j)),
                  _resident(cs.shape)],
        out_specs=bspec,
        out_shape=jax.ShapeDtypeStruct((b, tp, d), BF16),
        scratch_shapes=[pltpu.VMEM((2 * tp, ct), BF16)],
        compiler_params=_cparams(2), name="hyena_longconv")(
            v3, x03, p, q, skip.reshape(1, d), cs)


HI_MASK = 0xFFFF0000


def _pack_halves(x):
    w = x.shape[1] // 2
    lo = pltpu.bitcast(x[:, :w].astype(BF16).astype(F32), jnp.uint32)
    hi = pltpu.bitcast(x[:, w:].astype(BF16).astype(F32), jnp.uint32)
    return (lo >> 16) | (hi & jnp.uint32(HI_MASK))


def _unpack_halves(w):
    return (pltpu.bitcast(w << 16, F32), pltpu.bitcast(w & jnp.uint32(HI_MASK), F32))


def _router_kernel(x_ref, g_ref, wr_ref, route_ref, xp_ref):
    xn = _rms(x_ref[...], g_ref[...])
    xp_ref[...] = _pack_halves(xn)
    tm = xn.shape[0]
    lane = lax.broadcasted_iota(jnp.int32, (tm, LANES), 1)
    logits = jnp.where(lane < N_EXPERTS, _dot_f32(xn, wr_ref[...]), -jnp.inf)
    v1 = jnp.max(logits, axis=-1, keepdims=True)
    i1 = jnp.min(jnp.where(logits == v1, lane, LANES), axis=-1, keepdims=True)
    rest = jnp.where(lane == i1, -jnp.inf, logits)
    v2 = jnp.max(rest, axis=-1, keepdims=True)
    i2 = jnp.min(jnp.where(rest == v2, lane, LANES), axis=-1, keepdims=True)
    e2 = jnp.exp(v2 - v1)
    den = 1.0 + e2
    route_ref[...] = jnp.where(
        lane == 0, i1.astype(F32), jnp.where(
            lane == 1, i2.astype(F32), jnp.where(
                lane == 2, 1.0 / den, jnp.where(lane == 3, e2 / den, 0.0))))


def route_top2(x, g, router, tm=512):
    n_rows, d = x.shape
    tm = _row_tile(n_rows, tm)
    wr = jnp.pad(router, ((0, 0), (0, LANES - router.shape[1])))
    return pl.pallas_call(
        _router_kernel, grid=(n_rows // tm,),
        in_specs=[pl.BlockSpec((tm, d), lambda i: (i, 0)), _resident((1, d)), _resident(wr.shape)],
        out_specs=[pl.BlockSpec((tm, LANES), lambda i: (i, 0)),
                   pl.BlockSpec((tm, d // 2), lambda i: (i, 0))],
        out_shape=[jax.ShapeDtypeStruct((n_rows, LANES), F32),
                   jax.ShapeDtypeStruct((n_rows, d // 2), jnp.uint32)],
        compiler_params=_cparams(1), name="moe_router")(x, g.reshape(1, d), wr)


def route_plan(route, b, tp, tm):
    n = b * tp
    n_valid = b * (tp - PAD)
    n_tiles = (TOP_K * n_valid) // tm + N_EXPERTS
    r_max = n_tiles * tm
    tok_id = jnp.arange(n, dtype=jnp.int32)
    valid = (tok_id % tp) >= PAD
    experts = jnp.arange(N_EXPERTS, dtype=jnp.int32)
    choice = [route[:, k].astype(jnp.int32) for k in range(TOP_K)]
    hit = [(c[:, None] == experts[None, :]) & valid[:, None] for c in choice]
    member = (hit[0] | hit[1]).astype(jnp.int32)
    csum = jnp.cumsum(member, axis=0)
    rank = csum - member
    counts = csum[-1]
    tiles = (counts + tm - 1) // tm
    tile_end = jnp.cumsum(tiles)
    base = (tile_end - tiles) * tm
    pos = [jnp.where(valid, jnp.sum(jnp.where(h, base[None, :] + rank, 0), axis=1), r_max) for h in hit]
    pos = jnp.concatenate(pos)
    tok = jnp.zeros((r_max,), jnp.int32).at[pos].set(jnp.concatenate([tok_id, tok_id]), mode="drop")
    gate = jnp.zeros((r_max,), F32).at[pos].set(
        jnp.concatenate([route[:, TOP_K + k] for k in range(TOP_K)]), mode="drop")
    filled = jnp.zeros((r_max,), jnp.bool_).at[pos].set(True, mode="drop")
    filler = (jnp.cumsum(~filled) - 1).astype(jnp.int32) % (TOP_K * b * PAD)
    plane, within = filler // (b * PAD), filler % (b * PAD)
    trash = plane * n + (within // PAD) * tp + within % PAD
    slot = trash.at[pos].set(jnp.concatenate([tok_id, n + tok_id]), mode="drop")
    tile_id = jnp.arange(n_tiles, dtype=jnp.int32)
    tile_expert = jnp.minimum(jnp.sum(tile_id[:, None] >= tile_end[None, :], axis=1),
                              N_EXPERTS - 1).astype(jnp.int32)
    return (tok.reshape(n_tiles, 1, tm), slot.reshape(n_tiles, 1, tm), gate.reshape(r_max, 1),
            tile_expert, tile_end[-1:].astype(jnp.int32))


def _moe_sparse_kernel(te_ref, nu_ref, tok0_ref, tokn_ref, slot_ref, gate_ref, xp_hbm,
                       wg_ref, wu_ref, wd_ref, out_hbm, gbuf, ybuf, hm_ref, zbuf, gsem, ssem, zsem,
                       *, tf, tp):
    j = pl.program_id(0)
    n_used = nu_ref[0]
    tm = gbuf.shape[1]
    cur = j % 2

    @pl.when(j == 0)
    def _():
        zbuf[...] = jnp.zeros_like(zbuf)
        n_seq = out_hbm.shape[0] // tp
        copies = [pltpu.make_async_copy(zbuf, out_hbm.at[pl.ds(s * tp, PAD)], zsem)
                  for s in range(n_seq)]
        for c in copies:
            c.start()
        for c in copies:
            c.wait()

    def gather_rows(tok_ref, buf_slot):
        def body(r, carry):
            pltpu.make_async_copy(xp_hbm.at[pl.ds(tok_ref[0, 0, r], 1)],
                                  gbuf.at[buf_slot, pl.ds(r, 1)], gsem.at[buf_slot]).start()
            return carry
        lax.fori_loop(0, tm, body, 0, unroll=8)

    def wait_gather(buf_slot):
        pltpu.make_async_copy(xp_hbm.at[pl.ds(0, tm)], gbuf.at[buf_slot], gsem.at[buf_slot]).wait()

    def wait_scatter(buf_slot):
        pltpu.make_async_copy(ybuf.at[buf_slot], out_hbm.at[pl.ds(0, tm)], ssem.at[buf_slot]).wait()

    @pl.when((j == 0) & (n_used > 0))
    def _():
        gather_rows(tok0_ref, 0)

    @pl.when(j + 1 < n_used)
    def _():
        gather_rows(tokn_ref, 1 - cur)

    @pl.when(j < n_used)
    def _():
        wait_gather(cur)

        @pl.when(j >= 2)
        def _():
            wait_scatter(cur)

        lo, hi = _unpack_halves(gbuf[cur])
        x = jnp.concatenate([lo.astype(BF16), hi.astype(BF16)], axis=1)
        gate = gate_ref[...]
        f = wg_ref.shape[2]
        for c0 in range(0, f, tf):
            gt = _dot(x, wg_ref[0, :, c0:c0 + tf])
            ut = _dot(x, wu_ref[0, :, c0:c0 + tf])
            hm_ref[:, c0:c0 + tf] = (_silu(gt) * ut * gate).astype(BF16)
        ybuf[cur] = _pack_halves(_dot(hm_ref[...], wd_ref[0]))

        def body(r, carry):
            pltpu.make_async_copy(ybuf.at[cur, pl.ds(r, 1)],
                                  out_hbm.at[pl.ds(slot_ref[0, 0, r], 1)], ssem.at[cur]).start()
            return carry
        lax.fori_loop(0, tm, body, 0, unroll=8)

    @pl.when(j == pl.num_programs(0) - 1)
    def _():
        @pl.when(n_used >= 1)
        def _():
            wait_scatter((n_used - 1) % 2)

        @pl.when(n_used >= 2)
        def _():
            wait_scatter(n_used % 2)


def moe_sparse(xp, plan, wg, wu, wd, tp, tf=512):
    tok, slot, gate, tile_expert, n_used = plan
    n_tiles, _, tm = tok.shape
    n_rows, dh = xp.shape
    n_e, d, fe = wg.shape
    smem_tile = lambda imap: pl.BlockSpec((1, 1, tm), imap, memory_space=pltpu.SMEM)
    wspec = lambda shape: pl.BlockSpec(shape, lambda j, te, nu: (te[j], 0, 0),
                                       pipeline_mode=pl.Buffered(1))
    grid_spec = pltpu.PrefetchScalarGridSpec(
        num_scalar_prefetch=2, grid=(n_tiles,),
        in_specs=[smem_tile(lambda j, te, nu: (0, 0, 0)),
                  smem_tile(lambda j, te, nu: (jnp.minimum(j + 1, n_tiles - 1), 0, 0)),
                  smem_tile(lambda j, te, nu: (j, 0, 0)),
                  pl.BlockSpec((tm, 1), lambda j, te, nu: (j, 0)),
                  pl.BlockSpec(memory_space=pl.ANY),
                  wspec((1, d, fe)), wspec((1, d, fe)), wspec((1, fe, d))],
        out_specs=pl.BlockSpec(memory_space=pl.ANY),
        scratch_shapes=[pltpu.VMEM((2, tm, dh), jnp.uint32), pltpu.VMEM((2, tm, dh), jnp.uint32),
                        pltpu.VMEM((tm, fe), BF16), pltpu.VMEM((PAD, dh), jnp.uint32),
                        pltpu.SemaphoreType.DMA((2,)), pltpu.SemaphoreType.DMA((2,)),
                        pltpu.SemaphoreType.DMA(())])
    return pl.pallas_call(
        functools.partial(_moe_sparse_kernel, tf=tf, tp=tp), grid_spec=grid_spec,
        out_shape=jax.ShapeDtypeStruct((TOP_K * n_rows, dh), jnp.uint32),
        compiler_params=_cparams(1), name="moe_sparse")(
            tile_expert, n_used, tok, tok, slot, gate, xp, wg, wu, wd)


def _moe_combine_norm_kernel(h_ref, y0_ref, y1_ref, g_ref, o_ref):
    dh = y0_ref.shape[3]
    a_lo, a_hi = _unpack_halves(y0_ref[0, 0])
    b_lo, b_hi = _unpack_halves(y1_ref[0, 0])
    lo = h_ref[0, :, :dh] + a_lo + b_lo
    hi = h_ref[0, :, dh:] + a_hi + b_hi
    ms = (jnp.sum(lo * lo, axis=-1, keepdims=True)
          + jnp.sum(hi * hi, axis=-1, keepdims=True)) / (2 * dh)
    inv = lax.rsqrt(ms + EPS)
    o_ref[0, :, :dh] = lo * inv * g_ref[:, :dh]
    o_ref[0, :, dh:] = hi * inv * g_ref[:, dh:]


def moe_combine_final_norm(h3, y2, g, seq):
    b, tp, d = h3.shape
    assert seq % CHUNK == 0 and tp == seq + CHUNK
    y4 = y2.reshape(TOP_K, b, tp, d // 2)
    yspec = lambda k: pl.BlockSpec((1, 1, CHUNK, d // 2), lambda i, j: (k, i, j + 1, 0))
    return pl.pallas_call(
        _moe_combine_norm_kernel, grid=(b, seq // CHUNK),
        in_specs=[pl.BlockSpec((1, CHUNK, d), lambda i, j: (i, j + 1, 0)), yspec(0), yspec(1),
                  _resident((1, d))],
        out_specs=pl.BlockSpec((1, CHUNK, d), lambda i, j: (i, j, 0)),
        out_shape=jax.ShapeDtypeStruct((b, seq, d), F32),
        compiler_params=_cparams(2), name="moe_combine_norm")(h3, y4, y4, g.reshape(1, d))


def kernel(x, meta_tokens, norm_mix_even, w_in_ab, ssd_conv_w, ssd_conv_b, ssd_a_log,
           ssd_dt_bias, ssd_d, ssd_norm_w, diff_lambda, diff_subln_w, w_out_ab,
           norm_ffn_even, ffn_w_gate, ffn_w_up, ffn_w_down, norm_mix_odd, hy_w_in, hy_b_in,
           hy_conv_w, hy_conv_b, hy_f_w1, hy_f_b1, hy_f_w2, hy_f_b2, hy_f_w3, hy_f_b3,
           hy_f_w4, hy_f_freq, hy_skip, hy_w_out, hy_b_out, norm_ffn_odd, moe_router,
           moe_w_gate, moe_w_up, moe_w_down, final_norm):
    b, seq, d = x.shape
    assert d == D_MODEL and seq % 512 == 0
    tp = seq + CHUNK
    t_len = seq + N_META
    n = b * tp
    meta = jnp.broadcast_to(meta_tokens[None].astype(x.dtype), (b, N_META, d))
    h = jnp.concatenate([jnp.zeros((b, PAD, d), x.dtype), meta, x], axis=1).reshape(n, d)
    cos, sin = rope_tables(tp)
    depth = norm_mix_even.shape[0] + norm_mix_odd.shape[0]
    o1 = D_SSD + SSD_CONV_DIM
    o2 = o1 + 2 * SSD_HEADS
    for layer in range(depth):
        i = layer // 2
        if layer % 2 == 0:
            w = w_in_ab[i]
            w_dt = jnp.pad(w[:, o1:o2], ((0, 0), (0, LANES - 2 * SSD_HEADS)))
            zx, dt, qkv = rms_proj(
                h, norm_mix_even[i],
                [w[:, :o1].astype(BF16), w_dt.astype(BF16), w[:, o2:].astype(BF16)],
                None, [BF16, F32, BF16])
            ssd_out = ssd_mixer(zx.reshape(b, tp, -1), dt.reshape(b, tp, -1), ssd_conv_w[i],
                                ssd_conv_b[i], ssd_a_log[i], ssd_dt_bias[i], ssd_d[i],
                                ssd_norm_w[i])
            lambda_init = 0.8 - 0.6 * math.exp(-0.3 * layer)
            attn_out = diff_attention(qkv.reshape(b, tp, -1), cos, sin, diff_lambda[i],
                                      diff_subln_w[i], lambda_init)
            wo = w_out_ab[i].astype(BF16)
            h = proj_residual(h, [ssd_out.reshape(n, -1), attn_out.reshape(n, -1)],
                              [wo[:D_SSD], wo[D_SSD:]], None)
            h = ffn_residual(h, norm_ffn_even[i], ffn_w_gate[i].astype(BF16),
                             ffn_w_up[i].astype(BF16), ffn_w_down[i].astype(BF16))
        else:
            (p,) = rms_proj(h, norm_mix_odd[i], [hy_w_in[i].astype(BF16)],
                            [hy_b_in[i].reshape(1, -1)], [BF16])
            x0, v = hyena_conv_gate(p.reshape(b, tp, -1), hy_conv_w[i], hy_conv_b[i])
            hf, hb = hyena_filters(tp, t_len, hy_f_w1[i], hy_f_b1[i], hy_f_w2[i], hy_f_b2[i],
                                   hy_f_w3[i], hy_f_b3[i], hy_f_w4[i], hy_f_freq[i])
            cs = dft_tables(tp)
            fp, fq = hyena_spectrum(hf, hb, cs)
            y = hyena_longconv(v, x0, fp, fq, hy_skip[i], cs)
            h = proj_residual(h, [y.reshape(n, -1)], [hy_w_out[i].astype(BF16)], hy_b_out[i])
            assert layer == depth - 1, "an expert layer must close the trunk"
            route, xp = route_top2(h, norm_ffn_odd[i], moe_router[i])
            plan = route_plan(route, b, tp, MOE_TILE)
            y2 = moe_sparse(xp, plan, moe_w_gate[i].astype(BF16), moe_w_up[i].astype(BF16),
                            moe_w_down[i].astype(BF16), tp)
            return moe_combine_final_norm(h.reshape(b, tp, d), y2, final_norm, seq)
    return final_rmsnorm(h.reshape(b, tp, d), final_norm, seq)
```

```python
import functools
import math

import jax
import jax.numpy as jnp
from jax import lax
from jax.experimental import pallas as pl
from jax.experimental.pallas import tpu as pltpu

F32 = jnp.float32
BF16 = jnp.bfloat16

D_MODEL = 1024
N_META = 16
EPS = 1e-5
D_SSD = 512
SSD_HEADDIM = 64
SSD_HEADS = 8
SSD_GROUPS = 2
SSD_STATE = 128
SSD_CONV = 5
CHUNK = 128
SSD_CONV_DIM = D_SSD + 2 * SSD_GROUPS * SSD_STATE
D_DIFF = 512
DIFF_HEADDIM = 64
DIFF_HEADS = 4
ROPE_THETA = 500000.0
ROPE_DIM = 16
HYENA_SHORT = 3
HYENA_BANDS = 16
HYENA_SHIFT = 0.05
HYENA_TARGET = 1e-2
HYENA_FAST_DECAY = 0.3
HYENA_SLOW_DECAY = 1.5
N_EXPERTS = 8
TOP_K = 2

LANES = 128
PAD = CHUNK - N_META
VMEM_LIMIT = 48 * 1024 * 1024
MOE_TILE = 512


def _cparams(n_axes):
    return pltpu.CompilerParams(dimension_semantics=("arbitrary",) * n_axes,
                                vmem_limit_bytes=VMEM_LIMIT)


def _resident(shape):
    zeros = (0,) * len(shape)
    return pl.BlockSpec(shape, lambda *_: zeros, pipeline_mode=pl.Buffered(1))


def _row_tile(n_rows, preferred):
    assert n_rows % CHUNK == 0
    tm = preferred
    while n_rows % tm:
        tm //= 2
    return tm


def _dot(a, b):
    return jnp.dot(a, b, preferred_element_type=F32)


def _dot_nt(a, b):
    return lax.dot_general(a, b, (((1,), (1,)), ((), ())), preferred_element_type=F32)


def _split3(x):
    hi = x.astype(BF16)
    r = x - hi.astype(F32)
    mid = r.astype(BF16)
    lo = (r - mid.astype(F32)).astype(BF16)
    return hi, mid, lo


def _dot3_right(x, m):
    hi, mid, lo = _split3(x)
    return _dot(hi, m) + _dot(mid, m) + _dot(lo, m)


def _dot3_left(m, x):
    hi, mid, lo = _split3(x)
    return _dot(m, hi) + _dot(m, mid) + _dot(m, lo)


def _rms(x, w):
    return x * lax.rsqrt(jnp.mean(x * x, axis=-1, keepdims=True) + EPS) * w


def _silu(x):
    return x * jax.nn.sigmoid(x)


def _rms_proj_kernel(*refs, n_out, has_bias, col_chunk):
    x_ref, g_ref = refs[:2]
    w_refs = refs[2:2 + n_out]
    b_refs = refs[2 + n_out:2 + 2 * n_out] if has_bias else (None,) * n_out
    o_refs = refs[-n_out:]
    xn = _rms(x_ref[...], g_ref[...]).astype(BF16)
    for w_ref, b_ref, o_ref in zip(w_refs, b_refs, o_refs):
        n = o_ref.shape[-1]
        for c0 in range(0, n, col_chunk):
            c1 = min(n, c0 + col_chunk)
            y = _dot(xn, w_ref[:, c0:c1])
            if b_ref is not None:
                y = y + b_ref[:, c0:c1]
            o_ref[:, c0:c1] = y.astype(o_ref.dtype)


def rms_proj(x, g, ws, bs, out_dtypes, tm=512):
    n_rows, d = x.shape
    tm = _row_tile(n_rows, tm)
    n_out = len(ws)
    has_bias = bs is not None
    in_specs = [pl.BlockSpec((tm, d), lambda i: (i, 0)), _resident((1, d))]
    in_specs += [_resident(w.shape) for w in ws]
    args = [x, g.reshape(1, d)] + list(ws)
    if has_bias:
        in_specs += [_resident(b.shape) for b in bs]
        args += list(bs)
    out_shape = [jax.ShapeDtypeStruct((n_rows, w.shape[1]), dt) for w, dt in zip(ws, out_dtypes)]
    out_specs = [pl.BlockSpec((tm, w.shape[1]), lambda i: (i, 0)) for w in ws]
    return pl.pallas_call(
        functools.partial(_rms_proj_kernel, n_out=n_out, has_bias=has_bias, col_chunk=512),
        grid=(n_rows // tm,), in_specs=in_specs, out_specs=out_specs, out_shape=out_shape,
        compiler_params=_cparams(1), name="rms_proj")(*args)


def _proj_res_kernel(*refs, n_in, has_bias):
    res_ref = refs[0]
    a_refs = refs[1:1 + n_in]
    w_refs = refs[1 + n_in:1 + 2 * n_in]
    b_ref = refs[1 + 2 * n_in] if has_bias else None
    o_ref = refs[-1]
    acc = res_ref[...]
    for a_ref, w_ref in zip(a_refs, w_refs):
        acc = acc + _dot(a_ref[...], w_ref[...])
    if b_ref is not None:
        acc = acc + b_ref[...]
    o_ref[...] = acc


def proj_residual(res, a_list, w_list, bias, tm=512):
    n_rows, d = res.shape
    tm = _row_tile(n_rows, tm)
    n_in = len(a_list)
    has_bias = bias is not None
    in_specs = [pl.BlockSpec((tm, d), lambda i: (i, 0))]
    in_specs += [pl.BlockSpec((tm, a.shape[1]), lambda i: (i, 0)) for a in a_list]
    in_specs += [_resident(w.shape) for w in w_list]
    args = [res] + list(a_list) + list(w_list)
    if has_bias:
        in_specs.append(_resident((1, d)))
        args.append(bias.reshape(1, d))
    return pl.pallas_call(
        functools.partial(_proj_res_kernel, n_in=n_in, has_bias=has_bias),
        grid=(n_rows // tm,), in_specs=in_specs,
        out_specs=pl.BlockSpec((tm, d), lambda i: (i, 0)),
        out_shape=jax.ShapeDtypeStruct((n_rows, d), F32),
        compiler_params=_cparams(1), name="proj_residual")(*args)


def _ffn_kernel(x_ref, g_ref, wg_ref, wu_ref, wd_ref, o_ref, hm_ref, *, tf):
    x = x_ref[...]
    xn = _rms(x, g_ref[...]).astype(BF16)
    f = wg_ref.shape[1]
    for c0 in range(0, f, tf):
        gt = _dot(xn, wg_ref[:, c0:c0 + tf])
        ut = _dot(xn, wu_ref[:, c0:c0 + tf])
        hm_ref[:, c0:c0 + tf] = (_silu(gt) * ut).astype(BF16)
    o_ref[...] = x + _dot(hm_ref[...], wd_ref[...])


def ffn_residual(x, g, wg, wu, wd, tm=512, tf=256):
    n_rows, d = x.shape
    tm = _row_tile(n_rows, tm)
    f = wg.shape[1]
    return pl.pallas_call(
        functools.partial(_ffn_kernel, tf=tf),
        grid=(n_rows // tm,),
        in_specs=[pl.BlockSpec((tm, d), lambda i: (i, 0)), _resident((1, d)),
                  _resident(wg.shape), _resident(wu.shape), _resident(wd.shape)],
        out_specs=pl.BlockSpec((tm, d), lambda i: (i, 0)),
        out_shape=jax.ShapeDtypeStruct((n_rows, d), F32),
        scratch_shapes=[pltpu.VMEM((tm, f), BF16)],
        compiler_params=_cparams(1), name="ffn")(x, g.reshape(1, d), wg, wu, wd)


def _final_norm_kernel(x_ref, g_ref, o_ref):
    seq = o_ref.shape[1]
    for r0 in range(0, seq, 512):
        o_ref[0, r0:r0 + 512, :] = _rms(x_ref[0, CHUNK + r0:CHUNK + r0 + 512, :], g_ref[...])


def final_rmsnorm(h3, g, seq):
    b, tp, d = h3.shape
    assert seq % 512 == 0 and tp == seq + CHUNK
    return pl.pallas_call(
        _final_norm_kernel, grid=(b,),
        in_specs=[pl.BlockSpec((1, tp, d), lambda i: (i, 0, 0)), _resident((1, d))],
        out_specs=pl.BlockSpec((1, seq, d), lambda i: (i, 0, 0)),
        out_shape=jax.ShapeDtypeStruct((b, seq, d), F32),
        compiler_params=_cparams(1), name="final_norm")(h3, g.reshape(1, d))


def _rope_table_kernel(freq_ref, cos_ref, sin_ref):
    tp = cos_ref.shape[0]
    pos = (lax.broadcasted_iota(jnp.int32, (tp, LANES), 0) - PAD).astype(F32)
    lane = lax.broadcasted_iota(jnp.int32, (tp, LANES), 1) % DIFF_HEADDIM
    ang = pos * freq_ref[...]
    half = ROPE_DIM // 2
    rot = lane < ROPE_DIM
    cos_ref[...] = jnp.where(rot, jnp.cos(ang), 1.0)
    sin_ref[...] = jnp.where(lane < half, -jnp.sin(ang), jnp.where(rot, jnp.sin(ang), 0.0))


def rope_tables(tp):
    inv_freq = ROPE_THETA ** (-jnp.arange(0, ROPE_DIM, 2, dtype=F32) / ROPE_DIM)
    per_sub = jnp.concatenate([inv_freq, inv_freq, jnp.zeros((DIFF_HEADDIM - ROPE_DIM,), F32)])
    freq = jnp.concatenate([per_sub, per_sub]).reshape(1, LANES)
    return pl.pallas_call(
        _rope_table_kernel, grid=(1,),
        in_specs=[_resident((1, LANES))],
        out_specs=[pl.BlockSpec((tp, LANES), lambda i: (0, 0))] * 2,
        out_shape=[jax.ShapeDtypeStruct((tp, LANES), F32)] * 2,
        compiler_params=_cparams(1), name="rope_tables")(freq)


def _swap_matrix():
    src = lax.broadcasted_iota(jnp.int32, (LANES, LANES), 0)
    dst = lax.broadcasted_iota(jnp.int32, (LANES, LANES), 1)
    sub = dst % DIFF_HEADDIM
    half = ROPE_DIM // 2
    partner = jnp.where(sub < half, dst + half, jnp.where(sub < ROPE_DIM, dst - half, dst))
    return jnp.where(src == partner, 1.0, 0.0).astype(BF16)


def _rotate(x_bf16, cos, sin_signed, swap):
    return x_bf16.astype(F32) * cos + _dot(x_bf16, swap) * sin_signed


def _diff_attn_kernel(q_ref, k_ref, v_ref, cos_ref, sin_ref, lam_ref, w_ref, o_ref, kr_ref,
                      *, lambda_init, tq, unroll):
    tp = k_ref.shape[1]
    swap = _swap_matrix()
    kr_ref[...] = _rotate(k_ref[0], cos_ref[...], sin_ref[...], swap).astype(BF16)
    scale = DIFF_HEADDIM ** -0.5 * math.log2(math.e)
    sub_head = lax.broadcasted_iota(jnp.int32, (tq, LANES), 1) // DIFF_HEADDIM
    real_key = lax.broadcasted_iota(jnp.int32, (tq, CHUNK), 1) >= PAD
    lp = lam_ref[...]
    lam = (jnp.exp(jnp.sum(lp[0:1] * lp[1:2], axis=-1, keepdims=True))
           - jnp.exp(jnp.sum(lp[2:3] * lp[3:4], axis=-1, keepdims=True)) + lambda_init)

    def q_block(qi, carry):
        rows = pl.ds(pl.multiple_of(qi * tq, 16), tq)
        qr = (_rotate(q_ref[0, rows, :], cos_ref[rows, :], sin_ref[rows, :], swap) * scale).astype(BF16)
        maps = []
        for m in range(2):
            qm = jnp.where(sub_head == m, qr, jnp.zeros_like(qr))
            sa = jnp.where(real_key, _dot_nt(qm, kr_ref[0:CHUNK, :]), -jnp.inf)
            sb = _dot_nt(qm, kr_ref[CHUNK:, :])
            mx = jnp.maximum(jnp.max(sa, axis=-1, keepdims=True), jnp.max(sb, axis=-1, keepdims=True))
            pa = jnp.exp2(sa - mx)
            pb = jnp.exp2(sb - mx)
            den = jnp.sum(pa, axis=-1, keepdims=True) + jnp.sum(pb, axis=-1, keepdims=True)
            pv = _dot(pa.astype(BF16), v_ref[0, 0:CHUNK, :]) + _dot(pb.astype(BF16), v_ref[0, CHUNK:, :])
            maps.append(pv * (1.0 / den))
        o = maps[0] - lam * maps[1]
        o_ref[0, rows, :] = (_rms(o, w_ref[...]) * (1.0 - lambda_init)).astype(o_ref.dtype)
        return carry

    lax.fori_loop(0, tp // tq, q_block, 0, unroll=unroll)


def diff_attention(qkv3, cos, sin, lam_params, subln_w, lambda_init, tq=272, unroll=2):
    b, tp, _ = qkv3.shape
    hw = 2 * DIFF_HEADDIM
    assert hw == LANES and tp % tq == 0 and tq % 16 == 0
    head = lambda k: pl.BlockSpec((1, tp, hw), lambda i, h: (i, 0, k * DIFF_HEADS + h))
    return pl.pallas_call(
        functools.partial(_diff_attn_kernel, lambda_init=lambda_init, tq=tq, unroll=unroll),
        grid=(b, DIFF_HEADS),
        in_specs=[head(0), head(1), head(2), _resident((tp, LANES)), _resident((tp, LANES)),
                  _resident(lam_params.shape), _resident((1, hw))],
        out_specs=head(0),
        out_shape=jax.ShapeDtypeStruct((b, tp, D_DIFF), BF16),
        scratch_shapes=[pltpu.VMEM((tp, hw), BF16)],
        compiler_params=_cparams(2), name="diff_attention")(
            qkv3, qkv3, qkv3, cos, sin, lam_params, subln_w.reshape(1, hw))


def _softplus(x):
    return jnp.maximum(x, 0.0) + jnp.log1p(jnp.exp(-jnp.abs(x)))


def _ssd_kernel(zx_ref, dt_ref, cw_ref, cb_ref, dtb_ref, alog_ref, aexp_ref, dsk_ref, nw_ref,
                o_ref, xact_ref, yacc_ref, dtv_ref, st_ref):
    tp = zx_ref.shape[1]
    n_chunks = tp // CHUNK
    gn = SSD_GROUPS * SSD_STATE

    valid = lax.broadcasted_iota(jnp.int32, (tp, LANES), 0) >= PAD
    dsum = dsk_ref[0:1, :] + dsk_ref[1:2, :]
    for c0 in range(0, SSD_CONV_DIM, LANES):
        x = jnp.where(valid, zx_ref[0, :, D_SSD + c0:D_SSD + c0 + LANES].astype(F32), 0.0)
        acc = cb_ref[:, c0:c0 + LANES] + x * cw_ref[SSD_CONV // 2:SSD_CONV // 2 + 1, c0:c0 + LANES]
        for k in range(SSD_CONV):
            off = k - SSD_CONV // 2
            if off != 0:
                acc = acc + pltpu.roll(x, (-off) % tp, axis=0) * cw_ref[k:k + 1, c0:c0 + LANES]
        act = _silu(acc)
        xact_ref[:, c0:c0 + LANES] = act.astype(BF16)
        if c0 < D_SSD:
            yacc_ref[:, c0:c0 + LANES] = act * dsum[:, c0:c0 + LANES]
    dtv_ref[...] = jnp.where(valid, _softplus(dt_ref[0] + dtb_ref[...]), 0.0)
    st_ref[...] = jnp.zeros_like(st_ref)

    ii = lax.broadcasted_iota(jnp.int32, (CHUNK, CHUNK), 0)
    jj = lax.broadcasted_iota(jnp.int32, (CHUNK, CHUNK), 1)
    lower = jj <= ii
    upper = jj >= ii
    tril = jnp.where(lower, 1.0, 0.0).astype(BF16)
    triu = jnp.where(upper, 1.0, 0.0).astype(BF16)
    a_row = -jnp.exp(alog_ref[...])
    src_lane = lax.broadcasted_iota(jnp.int32, (LANES, D_SSD), 0)
    dst_head = lax.broadcasted_iota(jnp.int32, (LANES, D_SSD), 1) // SSD_HEADDIM
    half = lax.broadcasted_iota(jnp.int32, (CHUNK, LANES), 1) // SSD_HEADDIM

    def chunk_step(c, d):
        rows = pl.ds(pl.multiple_of(c * CHUNK, CHUNK), CHUNK)
        tri_col, tri_row, mask = (tril, triu, lower) if d == 0 else (triu, tril, upper)
        expand = jnp.where(src_lane == dst_head + SSD_HEADS * d, 1.0, 0.0).astype(BF16)
        dtc = dtv_ref[rows, :]
        a = dtc * a_row
        dt_exp = _dot3_right(dtc, expand)
        a_exp = dt_exp * (-jnp.exp(aexp_ref[d:d + 1, :]))
        col = _dot3_left(tri_col, a_exp)
        rowv = _dot3_right(a.T, tri_row)
        tot = col[CHUNK - 1:CHUNK, :] if d == 0 else col[0:1, :]
        ecol = jnp.exp(col)
        wst = jnp.exp(tot - col)
        dec = jnp.exp(tot)
        xdt = xact_ref[rows, 0:D_SSD].astype(F32) * dt_exp
        xdt_b = xdt.astype(BF16)
        xw = (xdt * wst).astype(BF16)
        for g in range(SSD_GROUPS):
            bg = xact_ref[rows, D_SSD + g * SSD_STATE:D_SSD + (g + 1) * SSD_STATE]
            cg = xact_ref[rows, D_SSD + gn + g * SSD_STATE:D_SSD + gn + (g + 1) * SSD_STATE]
            cb = _dot_nt(cg, bg)
            bt = bg.astype(F32).T.astype(BF16)
            for pp in range(2):
                p = 2 * g + pp
                lanes = slice(p * LANES, (p + 1) * LANES)
                s_prev = st_ref[d, p]
                y = _dot(cg, s_prev.astype(BF16)) * ecol[:, lanes]
                for hh in range(2):
                    h = 2 * p + hh
                    ccol = col[:, h * SSD_HEADDIM:h * SSD_HEADDIM + 1]
                    crow = rowv[h + SSD_HEADS * d:h + SSD_HEADS * d + 1, :]
                    lmat = jnp.exp(jnp.where(mask, ccol - crow, -jnp.inf))
                    rhs = jnp.where(half == hh, xdt_b[:, lanes], jnp.zeros((CHUNK, LANES), BF16))
                    y = y + _dot((cb * lmat).astype(BF16), rhs)
                yacc_ref[rows, lanes] += y
                st_ref[d, p] = s_prev * dec[:, lanes] + _dot(bt, xw[:, lanes])

    def scan_body(i, carry):
        chunk_step(i, 0)
        chunk_step(n_chunks - 1 - i, 1)
        return carry

    lax.fori_loop(0, n_chunks, scan_body, 0)

    gw = D_SSD // SSD_GROUPS

    def gate_body(c, carry):
        rows = pl.ds(pl.multiple_of(c * CHUNK, CHUNK), CHUNK)
        yz = yacc_ref[rows, :] * _silu(zx_ref[0, rows, 0:D_SSD].astype(F32))
        for g in range(SSD_GROUPS):
            seg = _rms(yz[:, g * gw:(g + 1) * gw], nw_ref[:, g * gw:(g + 1) * gw])
            o_ref[0, rows, g * gw:(g + 1) * gw] = seg.astype(o_ref.dtype)
        return carry

    lax.fori_loop(0, n_chunks, gate_body, 0)


def ssd_mixer(zx3, dt3, conv_w, conv_b, a_log, dt_bias, d_skip, norm_w):
    b, tp, wzx = zx3.shape
    lane_pad = LANES - 2 * SSD_HEADS
    dtb = jnp.pad(dt_bias.reshape(1, -1), ((0, 0), (0, lane_pad)))
    alog = jnp.pad(a_log.reshape(1, -1), ((0, 0), (0, lane_pad)))
    aexp = jnp.repeat(a_log, SSD_HEADDIM, axis=1)
    dsk = jnp.repeat(d_skip, SSD_HEADDIM, axis=1)
    return pl.pallas_call(
        _ssd_kernel, grid=(b,),
        in_specs=[pl.BlockSpec((1, tp, wzx), lambda i: (i, 0, 0)),
                  pl.BlockSpec((1, tp, LANES), lambda i: (i, 0, 0)),
                  _resident(conv_w.shape), _resident((1, SSD_CONV_DIM)),
                  _resident((1, LANES)), _resident((1, LANES)),
                  _resident((2, D_SSD)), _resident((2, D_SSD)), _resident((1, D_SSD))],
        out_specs=pl.BlockSpec((1, tp, D_SSD), lambda i: (i, 0, 0)),
        out_shape=jax.ShapeDtypeStruct((b, tp, D_SSD), BF16),
        scratch_shapes=[pltpu.VMEM((tp, SSD_CONV_DIM), BF16), pltpu.VMEM((tp, D_SSD), F32),
                        pltpu.VMEM((tp, LANES), F32),
                        pltpu.VMEM((2, SSD_HEADS // 2, SSD_STATE, LANES), F32)],
        compiler_params=_cparams(1), name="ssd_mixer")(
            zx3, dt3, conv_w, conv_b.reshape(1, -1), dtb, alog, aexp, dsk, norm_w.reshape(1, -1))


def _hy_conv_kernel(p0_ref, p1_ref, p2_ref, w0_ref, w1_ref, w2_ref, b0_ref, b1_ref, b2_ref,
                    x0_ref, v_ref):
    tp = p0_ref.shape[1]
    ct = p0_ref.shape[2]
    valid = lax.broadcasted_iota(jnp.int32, (tp, LANES), 0) >= PAD

    def conv(p_ref, w_ref, b_ref, c0):
        x = jnp.where(valid, p_ref[0, :, c0:c0 + LANES].astype(F32), 0.0)
        acc = b_ref[:, c0:c0 + LANES] + x * w_ref[HYENA_SHORT // 2:HYENA_SHORT // 2 + 1, c0:c0 + LANES]
        for k in range(HYENA_SHORT):
            off = k - HYENA_SHORT // 2
            if off != 0:
                acc = acc + pltpu.roll(x, (-off) % tp, axis=0) * w_ref[k:k + 1, c0:c0 + LANES]
        return acc

    for c0 in range(0, ct, LANES):
        x0_ref[0, :, c0:c0 + LANES] = conv(p0_ref, w0_ref, b0_ref, c0).astype(x0_ref.dtype)
        v = conv(p2_ref, w2_ref, b2_ref, c0) * conv(p1_ref, w1_ref, b1_ref, c0)
        v_ref[0, :, c0:c0 + LANES] = jnp.where(valid, v, 0.0).astype(v_ref.dtype)


def hyena_conv_gate(p3, conv_w, conv_b, ct=256):
    b, tp, d3 = p3.shape
    d = d3 // 3
    nct = d // ct
    conv_b = conv_b.reshape(1, d3)
    pspec = lambda k: pl.BlockSpec((1, tp, ct), lambda i, j: (i, 0, j + k * nct))
    wspec = lambda k: pl.BlockSpec((HYENA_SHORT, ct), lambda i, j: (0, j + k * nct))
    bspec = lambda k: pl.BlockSpec((1, ct), lambda i, j: (0, j + k * nct))
    ospec = pl.BlockSpec((1, tp, ct), lambda i, j: (i, 0, j))
    return pl.pallas_call(
        _hy_conv_kernel, grid=(b, nct),
        in_specs=[pspec(0), pspec(1), pspec(2), wspec(0), wspec(1), wspec(2),
                  bspec(0), bspec(1), bspec(2)],
        out_specs=[ospec, ospec],
        out_shape=[jax.ShapeDtypeStruct((b, tp, d), BF16)] * 2,
        compiler_params=_cparams(2), name="hyena_conv_gate")(
            p3, p3, p3, conv_w, conv_w, conv_w, conv_b, conv_b, conv_b)


def _dot_f32(a, b):
    return jnp.dot(a, b, precision=lax.Precision.HIGHEST, preferred_element_type=F32)


def _hy_filter_kernel(w1t_ref, w1c_ref, w1s_ref, b1_ref, w2_ref, b2_ref, w3_ref, b3_ref, fr_ref,
                      w4f_ref, w4b_ref, bands_ref, deltas_ref, hf_ref, hb_ref, *, t_len):
    tp = hf_ref.shape[0]
    row = lax.broadcasted_iota(jnp.int32, (tp, 1), 0)
    pos = row.astype(F32)
    t = pos / (t_len - 1)
    ang = (2.0 * math.pi * pos / t_len) * bands_ref[...]
    h = t * w1t_ref[...] + _dot_f32(jnp.cos(ang), w1c_ref[...]) + _dot_f32(-jnp.sin(ang), w1s_ref[...])
    h = jnp.sin(fr_ref[0:1, :] * (h + b1_ref[...]))
    h = jnp.sin(fr_ref[1:2, :] * (_dot_f32(h, w2_ref[...]) + b2_ref[...]))
    h = jnp.sin(fr_ref[2:3, :] * (_dot_f32(h, w3_ref[...]) + b3_ref[...]))
    decay = jnp.exp(-t * deltas_ref[...]) + HYENA_SHIFT
    hf = jnp.where(row < t_len, _dot_f32(h, w4f_ref[...]) * decay, 0.0)
    hb = jnp.where((row >= 1) & (row < t_len), _dot_f32(h, w4b_ref[...]) * decay, 0.0)
    norm = (jnp.sum(jnp.abs(hf), axis=0, keepdims=True)
            + jnp.sum(jnp.abs(hb), axis=0, keepdims=True))
    hf_ref[...] = hf / norm
    hb_ref[...] = hb / norm


def hyena_filters(tp, t_len, w1, b1, w2, b2, w3, b3, w4, freq, ct=256):
    d = w4.shape[1] // 2
    fw = w1.shape[1]
    nct = d // ct
    bands = jnp.linspace(1e-4, HYENA_BANDS - 1, HYENA_BANDS, dtype=F32).reshape(1, -1)
    max_decay = math.log(HYENA_TARGET) / HYENA_FAST_DECAY
    min_decay = math.log(HYENA_TARGET) / HYENA_SLOW_DECAY
    deltas = jnp.abs(jnp.linspace(min_decay, max_decay, d, dtype=F32)).reshape(1, -1)
    nb = HYENA_BANDS
    full = lambda a: pl.BlockSpec(a.shape, lambda j: (0,) * a.ndim)
    args = [w1[0:1], w1[1:1 + nb], w1[1 + nb:], b1.reshape(1, fw), w2, b2.reshape(1, fw),
            w3, b3.reshape(1, fw), freq]
    return pl.pallas_call(
        functools.partial(_hy_filter_kernel, t_len=t_len), grid=(nct,),
        in_specs=[full(a) for a in args] + [
            pl.BlockSpec((fw, ct), lambda j: (0, j)), pl.BlockSpec((fw, ct), lambda j: (0, j + nct)),
            full(bands), pl.BlockSpec((1, ct), lambda j: (0, j))],
        out_specs=[pl.BlockSpec((tp, ct), lambda j: (0, j))] * 2,
        out_shape=[jax.ShapeDtypeStruct((tp, d), F32)] * 2,
        compiler_params=_cparams(1), name="hyena_filters")(*args, w4, w4, bands, deltas)


def _dft_table_kernel(cs_ref, *, period):
    tr = cs_ref.shape[0]
    tp = cs_ref.shape[1] // 2
    a = lax.broadcasted_iota(jnp.int32, (tr, tp), 0) + pl.program_id(0) * tr
    b = lax.broadcasted_iota(jnp.int32, (tr, tp), 1)
    n = (2 * a + 1) * (2 * b + 1)
    q = jnp.floor(n.astype(F32) * (1.0 / period)).astype(jnp.int32)
    r = (n - q * period).astype(F32)
    ang = r * (2.0 * math.pi / period)
    cs_ref[:, :tp] = jnp.cos(ang).astype(cs_ref.dtype)
    cs_ref[:, tp:] = jnp.sin(ang).astype(cs_ref.dtype)


def dft_tables(tp, tr=272):
    return pl.pallas_call(
        functools.partial(_dft_table_kernel, period=8 * tp), grid=(tp // tr,),
        out_specs=pl.BlockSpec((tr, 2 * tp), lambda i: (i, 0)),
        out_shape=jax.ShapeDtypeStruct((tp, 2 * tp), BF16),
        compiler_params=_cparams(1), name="dft_tables")()


def _split2(x):
    hi = x.astype(BF16)
    return hi, (x - hi.astype(F32)).astype(BF16)


def _hy_spectrum_kernel(hf_ref, hb_ref, cs_ref, p_ref, q_ref, *, mc):
    tp = hf_ref.shape[0]
    big_l = 2 * tp
    scale = 2.0 / big_l
    hs_hi, hs_lo = _split2(hf_ref[...] + hb_ref[...])
    hd_hi, hd_lo = _split2(hf_ref[...] - hb_ref[...])
    for m0 in range(0, tp, mc):
        rows = slice(m0, m0 + mc)
        phi = ((lax.broadcasted_iota(jnp.int32, (mc, 1), 0) + m0).astype(F32) + 0.5) * (math.pi / big_l)
        cphi, sphi = jnp.cos(phi), jnp.sin(phi)
        cm, sm = cs_ref[rows, :tp], cs_ref[rows, tp:]
        c_hs = _dot(cm, hs_hi) + _dot(cm, hs_lo)
        s_hs = _dot(sm, hs_hi) + _dot(sm, hs_lo)
        c_hd = _dot(cm, hd_hi) + _dot(cm, hd_lo)
        s_hd = _dot(sm, hd_hi) + _dot(sm, hd_lo)
        p_ref[rows, :] = (c_hs * cphi + s_hs * sphi) * scale
        q_ref[rows, :] = (s_hd * cphi - c_hd * sphi) * scale


def hyena_spectrum(hf, hb, cs, ct=256, mc=272):
    tp, d = hf.shape
    cspec = pl.BlockSpec((tp, ct), lambda j: (0, j))
    return pl.pallas_call(
        functools.partial(_hy_spectrum_kernel, mc=mc), grid=(d // ct,),
        in_specs=[cspec, cspec, _resident(cs.shape)],
        out_specs=[cspec, cspec],
        out_shape=[jax.ShapeDtypeStruct((tp, d), F32)] * 2,
        compiler_params=_cparams(1), name="hyena_spectrum")(hf, hb, cs)


def _hy_longconv_kernel(v_ref, x0_ref, p_ref, q_ref, skip_ref, cs_ref, o_ref, ycs_ref, *, mc):
    tp = v_ref.shape[1]
    v = v_ref[0]
    for m0 in range(0, tp, mc):
        rows = slice(m0, m0 + mc)
        a = _dot(cs_ref[rows, :tp], v)
        b = _dot(cs_ref[rows, tp:], v)
        p, q = p_ref[rows, :], q_ref[rows, :]
        ycs_ref[m0:m0 + mc, :] = (a * p - b * q).astype(BF16)
        ycs_ref[tp + m0:tp + m0 + mc, :] = (a * q + b * p).astype(BF16)
    for m0 in range(0, tp, mc):
        rows = slice(m0, m0 + mc)
        y = _dot(cs_ref[rows, :], ycs_ref[...])
        y = (y + v_ref[0, rows, :].astype(F32) * skip_ref[...]) * x0_ref[0, rows, :].astype(F32)
        o_ref[0, rows, :] = y.astype(o_ref.dtype)


def hyena_longconv(v3, x03, p, q, skip, cs, ct=256, mc=272):
    b, tp, d = v3.shape
    bspec = pl.BlockSpec((1, tp, ct), lambda j, i: (i, 0, j))
    cspec = pl.BlockSpec((tp, ct), lambda j, i: (0, j))
    return pl.pallas_call(
        functools.partial(_hy_longconv_kernel, mc=mc), grid=(d // ct, b),
        in_specs=[bspec, bspec, cspec, cspec, pl.BlockSpec((1, ct), lambda j, i: (0, j)),
                  _resident(cs.shape)],
        out_specs=bspec,
        out_shape=jax.ShapeDtypeStruct((b, tp, d), BF16),
        scratch_shapes=[pltpu.VMEM((2 * tp, ct), BF16)],
        compiler_params=_cparams(2), name="hyena_longconv")(
            v3, x03, p, q, skip.reshape(1, d), cs)


HI_MASK = 0xFFFF0000


def _pack_halves(x):
    w = x.shape[1] // 2
    lo = pltpu.bitcast(x[:, :w].astype(BF16).astype(F32), jnp.uint32)
    hi = pltpu.bitcast(x[:, w:].astype(BF16).astype(F32), jnp.uint32)
    return (lo >> 16) | (hi & jnp.uint32(HI_MASK))


def _unpack_halves(w):
    return (pltpu.bitcast(w << 16, F32), pltpu.bitcast(w & jnp.uint32(HI_MASK), F32))


def _router_kernel(x_ref, g_ref, wr_ref, valid_ref, xp_ref, gate_ref, kind_ref, cnt_ref, carry_ref):
    @pl.when(pl.program_id(0) == 0)
    def _():
        carry_ref[...] = jnp.zeros_like(carry_ref)

    xn = _rms(x_ref[...], g_ref[...])
    xp_ref[...] = _pack_halves(xn)
    tm = xn.shape[0]
    lane = lax.broadcasted_iota(jnp.int32, (tm, LANES), 1)
    logits = jnp.where(lane < N_EXPERTS, _dot_f32(xn, wr_ref[...]), -jnp.inf)
    v1 = jnp.max(logits, axis=-1, keepdims=True)
    i1 = jnp.min(jnp.where(logits == v1, lane, LANES), axis=-1, keepdims=True)
    rest = jnp.where(lane == i1, -jnp.inf, logits)
    v2 = jnp.max(rest, axis=-1, keepdims=True)
    i2 = jnp.min(jnp.where(rest == v2, lane, LANES), axis=-1, keepdims=True)
    e2 = jnp.exp(v2 - v1)
    den = 1.0 + e2
    gates = jnp.where(lane == i1, 1.0 / den, 0.0) + jnp.where(lane == i2, e2 / den, 0.0)
    kind = jnp.where(lane == i1, 1.0, 0.0) + jnp.where(lane == i2, 2.0, 0.0)
    valid = valid_ref[...]
    gate_t = gates.T[0:N_EXPERTS, :] * valid
    kind_t = kind.T[0:N_EXPERTS, :] * valid
    gate_ref[...] = gate_t
    kind_ref[...] = kind_t
    member = jnp.where(kind_t > 0.0, 1.0, 0.0)
    earlier = (lax.broadcasted_iota(jnp.int32, (tm, tm), 0)
               <= lax.broadcasted_iota(jnp.int32, (tm, tm), 1))
    running = _dot(member.astype(BF16), jnp.where(earlier, 1.0, 0.0).astype(BF16))
    cnt_ref[...] = running + carry_ref[:, 0:1]
    carry_ref[...] = carry_ref[...] + jnp.sum(member, axis=-1, keepdims=True)


def route_top2(x, g, router, tp, tm=512):
    n_rows, d = x.shape
    tm = _row_tile(n_rows, tm)
    wr = jnp.pad(router, ((0, 0), (0, LANES - router.shape[1])))
    valid = ((jnp.arange(n_rows, dtype=jnp.int32) % tp) >= PAD).astype(F32).reshape(1, n_rows)
    espec = pl.BlockSpec((N_EXPERTS, tm), lambda i: (0, i))
    eshape = jax.ShapeDtypeStruct((N_EXPERTS, n_rows), F32)
    return pl.pallas_call(
        _router_kernel, grid=(n_rows // tm,),
        in_specs=[pl.BlockSpec((tm, d), lambda i: (i, 0)), _resident((1, d)), _resident(wr.shape),
                  pl.BlockSpec((1, tm), lambda i: (0, i))],
        out_specs=[pl.BlockSpec((tm, d // 2), lambda i: (i, 0)), espec, espec, espec],
        out_shape=[jax.ShapeDtypeStruct((n_rows, d // 2), jnp.uint32), eshape, eshape, eshape],
        scratch_shapes=[pltpu.VMEM((N_EXPERTS, LANES), F32)],
        compiler_params=_cparams(1), name="moe_router")(x, g.reshape(1, d), wr, valid)


def _route_invert_kernel(te_ref, q0_ref, lo_ref, hi_ref, cnt_ref, gate_ref, kind_ref, trash_ref,
                         tok_ref, slot_ref, gcol_ref, *, n_rows, n_win):
    j = pl.program_id(0)
    e = te_ref[j]
    tm = tok_ref.shape[2]
    want = (q0_ref[j] + 1 + lax.broadcasted_iota(jnp.int32, (tm, 1), 0)).astype(F32)
    field = lax.broadcasted_iota(jnp.int32, (LANES, LANES), 0)
    lane_id = lax.broadcasted_iota(jnp.int32, (LANES, LANES), 1).astype(F32)
    out_lane = lax.broadcasted_iota(jnp.int32, (tm, LANES), 1)

    def block(k, acc):
        cols = pl.ds(pl.multiple_of(k * LANES, LANES), LANES)
        c = cnt_ref[e, :, cols]
        kd = kind_ref[e, :, cols]
        g_hi, g_mid, g_lo = _split3(gate_ref[e, :, cols])
        onehot = jnp.where((c == want) & (kd > 0.0), 1.0, 0.0).astype(BF16)
        fields = jnp.where(
            field == 0, lane_id, jnp.where(
                field == 1, jnp.where(kd == 2.0, 1.0, 0.0), jnp.where(
                    field == 2, g_hi.astype(F32), jnp.where(
                        field == 3, g_mid.astype(F32), jnp.where(
                            field == 4, g_lo.astype(F32), jnp.where(
                                field == 5, 1.0, jnp.where(
                                    field == 6, (k // LANES).astype(F32), jnp.where(
                                        field == 7, (k % LANES).astype(F32), 0.0))))))))
        return acc + _dot_nt(onehot, fields.astype(BF16))

    acc = lax.fori_loop(lo_ref[j], hi_ref[j], block, jnp.zeros((tm, LANES), F32))
    gcol_ref[...] = jnp.sum(jnp.where((out_lane >= 2) & (out_lane <= 4), acc, 0.0),
                            axis=-1, keepdims=True)
    t = acc.T
    found = t[5:6, :] > 0.5
    tok = jnp.where(found, (t[6:7, :] * LANES + t[7:8, :]) * LANES + t[0:1, :], 0.0).astype(jnp.int32)
    plane = jnp.where(t[1:2, :] > 0.5, n_rows, 0)
    trash = trash_ref[:, pl.ds(pl.multiple_of((j % n_win) * tm, tm), tm)]
    tok_ref[0] = tok
    slot_ref[0] = jnp.where(found, plane + tok, trash)


def route_plan(gate_t, kind_t, cnt_t, b, tp, tm):
    n = b * tp
    n_tiles = (TOP_K * b * (tp - PAD)) // tm + N_EXPERTS
    n_blk = n // LANES
    counts = cnt_t[:, -1].astype(jnp.int32)
    tiles = (counts + tm - 1) // tm
    tile_end = jnp.cumsum(tiles)
    tile_id = jnp.arange(n_tiles, dtype=jnp.int32)
    te = jnp.minimum(jnp.sum(tile_id[:, None] >= tile_end[None, :], axis=1), N_EXPERTS - 1)
    te = te.astype(jnp.int32)
    q0 = (tile_id - (tile_end - tiles)[te]) * tm
    block_end = cnt_t[:, LANES - 1::LANES].astype(jnp.int32)[te]
    last = jnp.minimum(q0 + tm, counts[te])
    lo = jnp.sum(block_end < (q0 + 1)[:, None], axis=1)
    hi = jnp.minimum(jnp.sum(block_end < last[:, None], axis=1) + 1, n_blk)
    hi = jnp.maximum(hi, lo)
    n_pad_rows = TOP_K * b * PAD
    n_win = max(1, n_pad_rows // tm)
    p = jnp.arange(n_win * tm, dtype=jnp.int32) % n_pad_rows
    trash = ((p // (b * PAD)) * n + ((p % (b * PAD)) // PAD) * tp + p % PAD).reshape(1, n_win * tm)
    grid_spec = pltpu.PrefetchScalarGridSpec(
        num_scalar_prefetch=4, grid=(n_tiles,),
        in_specs=[_resident((N_EXPERTS, 1, n))] * 3 + [_resident(trash.shape)],
        out_specs=[pl.BlockSpec((1, 1, tm), lambda j, *_: (j, 0, 0)),
                   pl.BlockSpec((1, 1, tm), lambda j, *_: (j, 0, 0)),
                   pl.BlockSpec((tm, 1), lambda j, *_: (j, 0))])
    tok, slot, gate = pl.pallas_call(
        functools.partial(_route_invert_kernel, n_rows=n, n_win=n_win), grid_spec=grid_spec,
        out_shape=[jax.ShapeDtypeStruct((n_tiles, 1, tm), jnp.int32),
                   jax.ShapeDtypeStruct((n_tiles, 1, tm), jnp.int32),
                   jax.ShapeDtypeStruct((n_tiles * tm, 1), F32)],
        compiler_params=_cparams(1), name="route_invert")(
            te, q0.astype(jnp.int32), lo.astype(jnp.int32), hi.astype(jnp.int32),
            cnt_t.reshape(N_EXPERTS, 1, n), gate_t.reshape(N_EXPERTS, 1, n),
            kind_t.reshape(N_EXPERTS, 1, n), trash)
    trash_tile = trash[:, (n_win - 1) * tm:].reshape(1, 1, tm)
    return tok, slot, gate, te, trash_tile


def _moe_sparse_kernel(te_ref, tokc_ref, tok1_ref, slotp_ref, slotc_ref, gate_ref, xp_hbm,
                       wg_ref, wu_ref, wd_ref, out_hbm, gbuf, ybuf, hm_ref, zbuf, gsem, ssem, zsem,
                       *, tf, tp):
    j = pl.program_id(0)
    tm = gbuf.shape[1]
    cur = j % 2

    @pl.when(j == 0)
    def _():
        zbuf[...] = jnp.zeros_like(zbuf)
        n_seq = out_hbm.shape[0] // tp
        copies = [pltpu.make_async_copy(zbuf, out_hbm.at[pl.ds(s * tp, PAD)], zsem)
                  for s in range(n_seq)]
        for c in copies:
            c.start()
        for c in copies:
            c.wait()

    prev = 1 - cur
    last = pl.num_programs(0) - 1

    def gather_row(tok_ref, buf_slot, r):
        return pltpu.make_async_copy(xp_hbm.at[pl.ds(tok_ref[0, 0, r], 1)],
                                     gbuf.at[buf_slot, pl.ds(r, 1)], gsem.at[buf_slot])

    def scatter_row(dst_ref, buf_slot, r):
        return pltpu.make_async_copy(ybuf.at[buf_slot, pl.ds(r, 1)],
                                     out_hbm.at[pl.ds(dst_ref[0, 0, r], 1)], ssem.at[buf_slot])

    def wait_gather(buf_slot):
        pltpu.make_async_copy(xp_hbm.at[pl.ds(0, tm)], gbuf.at[buf_slot], gsem.at[buf_slot]).wait()

    def wait_scatter(buf_slot):
        pltpu.make_async_copy(ybuf.at[buf_slot], out_hbm.at[pl.ds(0, tm)], ssem.at[buf_slot]).wait()

    @pl.when(j == 0)
    def _():
        ybuf[1] = jnp.zeros((tm, ybuf.shape[2]), ybuf.dtype)

        def body(r, carry):
            gather_row(tokc_ref, 0, r).start()
            return carry
        lax.fori_loop(0, tm, body, 0, unroll=8)

    wait_gather(cur)
    lo, hi = _unpack_halves(gbuf[cur])
    x = jnp.concatenate([lo.astype(BF16), hi.astype(BF16)], axis=1)
    gate = gate_ref[...]
    f = wg_ref.shape[2]
    n_chunks = f // tf
    per_chunk = -(-tm // n_chunks)
    for ci in range(n_chunks):
        c0 = ci * tf
        gt = _dot(x, wg_ref[0, :, c0:c0 + tf])
        ut = _dot(x, wu_ref[0, :, c0:c0 + tf])
        hm_ref[:, c0:c0 + tf] = (_silu(gt) * ut * gate).astype(BF16)
        for r in range(ci * per_chunk, min(tm, (ci + 1) * per_chunk)):
            gather_row(tok1_ref, prev, r).start()
            scatter_row(slotp_ref, prev, r).start()
    y = _pack_halves(_dot(hm_ref[...], wd_ref[0]))

    @pl.when(j >= 1)
    def _():
        wait_scatter(cur)

    ybuf[cur] = y

    @pl.when(j == last)
    def _():
        def body(r, carry):
            scatter_row(slotc_ref, cur, r).start()
            return carry
        lax.fori_loop(0, tm, body, 0, unroll=8)
        wait_scatter(prev)
        wait_scatter(cur)
        wait_gather(prev)


def moe_sparse(xp, plan, wg, wu, wd, tp, tf=512):
    tok, slot, gate, tile_expert, trash_tile = plan
    n_tiles, _, tm = tok.shape
    n_rows, dh = xp.shape
    n_e, d, fe = wg.shape
    slot = jnp.concatenate([trash_tile, slot])
    smem_tile = lambda imap: pl.BlockSpec((1, 1, tm), imap, memory_space=pltpu.SMEM)
    wspec = lambda shape: pl.BlockSpec(shape, lambda j, te: (te[j], 0, 0),
                                       pipeline_mode=pl.Buffered(1))
    grid_spec = pltpu.PrefetchScalarGridSpec(
        num_scalar_prefetch=1, grid=(n_tiles,),
        in_specs=[smem_tile(lambda j, te: (j, 0, 0)),
                  smem_tile(lambda j, te: (jnp.minimum(j + 1, n_tiles - 1), 0, 0)),
                  smem_tile(lambda j, te: (j, 0, 0)),
                  smem_tile(lambda j, te: (j + 1, 0, 0)),
                  pl.BlockSpec((tm, 1), lambda j, te: (j, 0)),
                  pl.BlockSpec(memory_space=pl.ANY),
                  wspec((1, d, fe)), wspec((1, d, fe)), wspec((1, fe, d))],
        out_specs=pl.BlockSpec(memory_space=pl.ANY),
        scratch_shapes=[pltpu.VMEM((2, tm, dh), jnp.uint32), pltpu.VMEM((2, tm, dh), jnp.uint32),
                        pltpu.VMEM((tm, fe), BF16), pltpu.VMEM((PAD, dh), jnp.uint32),
                        pltpu.SemaphoreType.DMA((2,)), pltpu.SemaphoreType.DMA((2,)),
                        pltpu.SemaphoreType.DMA(())])
    return pl.pallas_call(
        functools.partial(_moe_sparse_kernel, tf=tf, tp=tp), grid_spec=grid_spec,
        out_shape=jax.ShapeDtypeStruct((TOP_K * n_rows, dh), jnp.uint32),
        compiler_params=_cparams(1), name="moe_sparse")(
            tile_expert, tok, tok, slot, slot, gate, xp, wg, wu, wd)


def _moe_combine_norm_kernel(h_ref, y0_ref, y1_ref, g_ref, o_ref):
    dh = y0_ref.shape[3]
    a_lo, a_hi = _unpack_halves(y0_ref[0, 0])
    b_lo, b_hi = _unpack_halves(y1_ref[0, 0])
    lo = h_ref[0, :, :dh] + a_lo + b_lo
    hi = h_ref[0, :, dh:] + a_hi + b_hi
    ms = (jnp.sum(lo * lo, axis=-1, keepdims=True)
          + jnp.sum(hi * hi, axis=-1, keepdims=True)) / (2 * dh)
    inv = lax.rsqrt(ms + EPS)
    o_ref[0, :, :dh] = lo * inv * g_ref[:, :dh]
    o_ref[0, :, dh:] = hi * inv * g_ref[:, dh:]


def moe_combine_final_norm(h3, y2, g, seq):
    b, tp, d = h3.shape
    assert seq % CHUNK == 0 and tp == seq + CHUNK
    y4 = y2.reshape(TOP_K, b, tp, d // 2)
    yspec = lambda k: pl.BlockSpec((1, 1, CHUNK, d // 2), lambda i, j: (k, i, j + 1, 0))
    return pl.pallas_call(
        _moe_combine_norm_kernel, grid=(b, seq // CHUNK),
        in_specs=[pl.BlockSpec((1, CHUNK, d), lambda i, j: (i, j + 1, 0)), yspec(0), yspec(1),
                  _resident((1, d))],
        out_specs=pl.BlockSpec((1, CHUNK, d), lambda i, j: (i, j, 0)),
        out_shape=jax.ShapeDtypeStruct((b, seq, d), F32),
        compiler_params=_cparams(2), name="moe_combine_norm")(h3, y4, y4, g.reshape(1, d))


def kernel(x, meta_tokens, norm_mix_even, w_in_ab, ssd_conv_w, ssd_conv_b, ssd_a_log,
           ssd_dt_bias, ssd_d, ssd_norm_w, diff_lambda, diff_subln_w, w_out_ab,
           norm_ffn_even, ffn_w_gate, ffn_w_up, ffn_w_down, norm_mix_odd, hy_w_in, hy_b_in,
           hy_conv_w, hy_conv_b, hy_f_w1, hy_f_b1, hy_f_w2, hy_f_b2, hy_f_w3, hy_f_b3,
           hy_f_w4, hy_f_freq, hy_skip, hy_w_out, hy_b_out, norm_ffn_odd, moe_router,
           moe_w_gate, moe_w_up, moe_w_down, final_norm):
    b, seq, d = x.shape
    assert d == D_MODEL and seq % 512 == 0
    tp = seq + CHUNK
    t_len = seq + N_META
    n = b * tp
    meta = jnp.broadcast_to(meta_tokens[None].astype(x.dtype), (b, N_META, d))
    h = jnp.concatenate([jnp.zeros((b, PAD, d), x.dtype), meta, x], axis=1).reshape(n, d)
    cos, sin = rope_tables(tp)
    depth = norm_mix_even.shape[0] + norm_mix_odd.shape[0]
    o1 = D_SSD + SSD_CONV_DIM
    o2 = o1 + 2 * SSD_HEADS
    for layer in range(depth):
        i = layer // 2
        if layer % 2 == 0:
            w = w_in_ab[i]
            w_dt = jnp.pad(w[:, o1:o2], ((0, 0), (0, LANES - 2 * SSD_HEADS)))
            zx, dt, qkv = rms_proj(
                h, norm_mix_even[i],
                [w[:, :o1].astype(BF16), w_dt.astype(BF16), w[:, o2:].astype(BF16)],
                None, [BF16, F32, BF16])
            ssd_out = ssd_mixer(zx.reshape(b, tp, -1), dt.reshape(b, tp, -1), ssd_conv_w[i],
                                ssd_conv_b[i], ssd_a_log[i], ssd_dt_bias[i], ssd_d[i],
                                ssd_norm_w[i])
            lambda_init = 0.8 - 0.6 * math.exp(-0.3 * layer)
            attn_out = diff_attention(qkv.reshape(b, tp, -1), cos, sin, diff_lambda[i],
                                      diff_subln_w[i], lambda_init)
            wo = w_out_ab[i].astype(BF16)
            h = proj_residual(h, [ssd_out.reshape(n, -1), attn_out.reshape(n, -1)],
                              [wo[:D_SSD], wo[D_SSD:]], None)
            h = ffn_residual(h, norm_ffn_even[i], ffn_w_gate[i].astype(BF16),
                             ffn_w_up[i].astype(BF16), ffn_w_down[i].astype(BF16))
        else:
            (p,) = rms_proj(h, norm_mix_odd[i], [hy_w_in[i].astype(BF16)],
                            [hy_b_in[i].reshape(1, -1)], [BF16])
            x0, v = hyena_conv_gate(p.reshape(b, tp, -1), hy_conv_w[i], hy_conv_b[i])
            hf, hb = hyena_filters(tp, t_len, hy_f_w1[i], hy_f_b1[i], hy_f_w2[i], hy_f_b2[i],
                                   hy_f_w3[i], hy_f_b3[i], hy_f_w4[i], hy_f_freq[i])
            cs = dft_tables(tp)
            fp, fq = hyena_spectrum(hf, hb, cs)
            y = hyena_longconv(v, x0, fp, fq, hy_skip[i], cs)
            h = proj_residual(h, [y.reshape(n, -1)], [hy_w_out[i].astype(BF16)], hy_b_out[i])
            assert layer == depth - 1, "an expert layer must close the trunk"
            xp, gate_t, kind_t, cnt_t = route_top2(h, norm_ffn_odd[i], moe_router[i], tp)
            plan = route_plan(gate_t, kind_t, cnt_t, b, tp, MOE_TILE)
            y2 = moe_sparse(xp, plan, moe_w_gate[i].astype(BF16), moe_w_up[i].astype(BF16),
                            moe_w_down[i].astype(BF16), tp)
            return moe_combine_final_norm(h.reshape(b, tp, d), y2, final_norm, seq)
    return final_rmsnorm(h.reshape(b, tp, d), final_norm, seq)
```

```python
import functools
import math

import jax
import jax.numpy as jnp
from jax import lax
from jax.experimental import pallas as pl
from jax.experimental.pallas import tpu as pltpu

F32 = jnp.float32
BF16 = jnp.bfloat16

D_MODEL = 1024
N_META = 16
EPS = 1e-5
D_SSD = 512
SSD_HEADDIM = 64
SSD_HEADS = 8
SSD_GROUPS = 2
SSD_STATE = 128
SSD_CONV = 5
CHUNK = 128
SSD_CONV_DIM = D_SSD + 2 * SSD_GROUPS * SSD_STATE
D_DIFF = 512
DIFF_HEADDIM = 64
DIFF_HEADS = 4
ROPE_THETA = 500000.0
ROPE_DIM = 16
HYENA_SHORT = 3
HYENA_BANDS = 16
HYENA_SHIFT = 0.05
HYENA_TARGET = 1e-2
HYENA_FAST_DECAY = 0.3
HYENA_SLOW_DECAY = 1.5
N_EXPERTS = 8
TOP_K = 2

LANES = 128
PAD = CHUNK - N_META
VMEM_LIMIT = 48 * 1024 * 1024
MOE_TILE = 512
SUB_ROWS = 64


def _cparams(n_axes):
    return pltpu.CompilerParams(dimension_semantics=("arbitrary",) * n_axes,
                                vmem_limit_bytes=VMEM_LIMIT)


def _resident(shape):
    zeros = (0,) * len(shape)
    return pl.BlockSpec(shape, lambda *_: zeros, pipeline_mode=pl.Buffered(1))


def _row_tile(n_rows, preferred):
    assert n_rows % CHUNK == 0
    tm = preferred
    while n_rows % tm:
        tm //= 2
    return tm


def _dot(a, b):
    return jnp.dot(a, b, preferred_element_type=F32)


def _dot_nt(a, b):
    return lax.dot_general(a, b, (((1,), (1,)), ((), ())), preferred_element_type=F32)


def _split3(x):
    hi = x.astype(BF16)
    r = x - hi.astype(F32)
    mid = r.astype(BF16)
    lo = (r - mid.astype(F32)).astype(BF16)
    return hi, mid, lo


def _dot3_right(x, m):
    hi, mid, lo = _split3(x)
    return _dot(hi, m) + _dot(mid, m) + _dot(lo, m)


def _dot3_left(m, x):
    hi, mid, lo = _split3(x)
    return _dot(m, hi) + _dot(m, mid) + _dot(m, lo)


def _rms(x, w):
    return x * lax.rsqrt(jnp.mean(x * x, axis=-1, keepdims=True) + EPS) * w


def _silu(x):
    return x * jax.nn.sigmoid(x)


def _rms_proj_kernel(*refs, n_out, has_bias, col_chunk):
    x_ref, g_ref = refs[:2]
    w_refs = refs[2:2 + n_out]
    b_refs = refs[2 + n_out:2 + 2 * n_out] if has_bias else (None,) * n_out
    o_refs = refs[-n_out:]
    xn = _rms(x_ref[...], g_ref[...]).astype(BF16)
    for w_ref, b_ref, o_ref in zip(w_refs, b_refs, o_refs):
        n = o_ref.shape[-1]
        for c0 in range(0, n, col_chunk):
            c1 = min(n, c0 + col_chunk)
            y = _dot(xn, w_ref[:, c0:c1])
            if b_ref is not None:
                y = y + b_ref[:, c0:c1]
            o_ref[:, c0:c1] = y.astype(o_ref.dtype)


def rms_proj(x, g, ws, bs, out_dtypes, tm=512):
    n_rows, d = x.shape
    tm = _row_tile(n_rows, tm)
    n_out = len(ws)
    has_bias = bs is not None
    in_specs = [pl.BlockSpec((tm, d), lambda i: (i, 0)), _resident((1, d))]
    in_specs += [_resident(w.shape) for w in ws]
    args = [x, g.reshape(1, d)] + list(ws)
    if has_bias:
        in_specs += [_resident(b.shape) for b in bs]
        args += list(bs)
    out_shape = [jax.ShapeDtypeStruct((n_rows, w.shape[1]), dt) for w, dt in zip(ws, out_dtypes)]
    out_specs = [pl.BlockSpec((tm, w.shape[1]), lambda i: (i, 0)) for w in ws]
    return pl.pallas_call(
        functools.partial(_rms_proj_kernel, n_out=n_out, has_bias=has_bias, col_chunk=512),
        grid=(n_rows // tm,), in_specs=in_specs, out_specs=out_specs, out_shape=out_shape,
        compiler_params=_cparams(1), name="rms_proj")(*args)


def _ffn_kernel(*refs, n_in, tf):
    x_ref = refs[0]
    a_refs = refs[1:1 + n_in]
    wo_refs = refs[1 + n_in:1 + 2 * n_in]
    g_ref, wg_ref, wu_ref, wd_ref, o_ref, hm_ref = refs[1 + 2 * n_in:]
    x = x_ref[...]
    for a_ref, wo_ref in zip(a_refs, wo_refs):
        x = x + _dot(a_ref[...], wo_ref[...])
    xn = _rms(x, g_ref[...]).astype(BF16)
    f = wg_ref.shape[1]
    for c0 in range(0, f, tf):
        gt = _dot(xn, wg_ref[:, c0:c0 + tf])
        ut = _dot(xn, wu_ref[:, c0:c0 + tf])
        hm_ref[:, c0:c0 + tf] = (_silu(gt) * ut).astype(BF16)
    o_ref[...] = x + _dot(hm_ref[...], wd_ref[...])


def ffn_residual(x, a_list, wo_list, g, wg, wu, wd, tm=512, tf=256):
    n_rows, d = x.shape
    tm = _row_tile(n_rows, tm)
    f = wg.shape[1]
    row = lambda w: pl.BlockSpec((tm, w), lambda i: (i, 0))
    return pl.pallas_call(
        functools.partial(_ffn_kernel, n_in=len(a_list), tf=tf),
        grid=(n_rows // tm,),
        in_specs=[row(d)] + [row(a.shape[1]) for a in a_list] + [_resident(w.shape) for w in wo_list]
        + [_resident((1, d)), _resident(wg.shape), _resident(wu.shape), _resident(wd.shape)],
        out_specs=row(d),
        out_shape=jax.ShapeDtypeStruct((n_rows, d), F32),
        scratch_shapes=[pltpu.VMEM((tm, f), BF16)],
        compiler_params=_cparams(1), name="ffn")(
            x, *a_list, *wo_list, g.reshape(1, d), wg, wu, wd)


def _final_norm_kernel(x_ref, g_ref, o_ref):
    seq = o_ref.shape[1]
    for r0 in range(0, seq, 512):
        o_ref[0, r0:r0 + 512, :] = _rms(x_ref[0, CHUNK + r0:CHUNK + r0 + 512, :], g_ref[...])


def final_rmsnorm(h3, g, seq):
    b, tp, d = h3.shape
    assert seq % 512 == 0 and tp == seq + CHUNK
    return pl.pallas_call(
        _final_norm_kernel, grid=(b,),
        in_specs=[pl.BlockSpec((1, tp, d), lambda i: (i, 0, 0)), _resident((1, d))],
        out_specs=pl.BlockSpec((1, seq, d), lambda i: (i, 0, 0)),
        out_shape=jax.ShapeDtypeStruct((b, seq, d), F32),
        compiler_params=_cparams(1), name="final_norm")(h3, g.reshape(1, d))


def _rope_table_kernel(freq_ref, cos_ref, sin_ref):
    tp = cos_ref.shape[0]
    pos = (lax.broadcasted_iota(jnp.int32, (tp, LANES), 0) - PAD).astype(F32)
    lane = lax.broadcasted_iota(jnp.int32, (tp, LANES), 1) % DIFF_HEADDIM
    ang = pos * freq_ref[...]
    half = ROPE_DIM // 2
    rot = lane < ROPE_DIM
    cos_ref[...] = jnp.where(rot, jnp.cos(ang), 1.0)
    sin_ref[...] = jnp.where(lane < half, -jnp.sin(ang), jnp.where(rot, jnp.sin(ang), 0.0))


def rope_tables(tp):
    inv_freq = ROPE_THETA ** (-jnp.arange(0, ROPE_DIM, 2, dtype=F32) / ROPE_DIM)
    per_sub = jnp.concatenate([inv_freq, inv_freq, jnp.zeros((DIFF_HEADDIM - ROPE_DIM,), F32)])
    freq = jnp.concatenate([per_sub, per_sub]).reshape(1, LANES)
    return pl.pallas_call(
        _rope_table_kernel, grid=(1,),
        in_specs=[_resident((1, LANES))],
        out_specs=[pl.BlockSpec((tp, LANES), lambda i: (0, 0))] * 2,
        out_shape=[jax.ShapeDtypeStruct((tp, LANES), F32)] * 2,
        compiler_params=_cparams(1), name="rope_tables")(freq)


def _swap_matrix():
    src = lax.broadcasted_iota(jnp.int32, (LANES, LANES), 0)
    dst = lax.broadcasted_iota(jnp.int32, (LANES, LANES), 1)
    sub = dst % DIFF_HEADDIM
    half = ROPE_DIM // 2
    partner = jnp.where(sub < half, dst + half, jnp.where(sub < ROPE_DIM, dst - half, dst))
    return jnp.where(src == partner, 1.0, 0.0).astype(BF16)


def _rotate(x_bf16, cos, sin_signed, swap):
    return x_bf16.astype(F32) * cos + _dot(x_bf16, swap) * sin_signed


def _diff_attn_kernel(q_ref, k_ref, v_ref, cos_ref, sin_ref, lam_ref, w_ref, o_ref, kr_ref,
                      *, lambda_init, tq, unroll):
    tp = k_ref.shape[1]
    swap = _swap_matrix()
    kr_ref[...] = _rotate(k_ref[0], cos_ref[...], sin_ref[...], swap).astype(BF16)
    scale = DIFF_HEADDIM ** -0.5 * math.log2(math.e)
    sub_head = lax.broadcasted_iota(jnp.int32, (tq, LANES), 1) // DIFF_HEADDIM
    real_key = lax.broadcasted_iota(jnp.int32, (tq, CHUNK), 1) >= PAD
    lp = lam_ref[...]
    lam = (jnp.exp(jnp.sum(lp[0:1] * lp[1:2], axis=-1, keepdims=True))
           - jnp.exp(jnp.sum(lp[2:3] * lp[3:4], axis=-1, keepdims=True)) + lambda_init)

    def q_block(qi, carry):
        rows = pl.ds(pl.multiple_of(qi * tq, 16), tq)
        qr = (_rotate(q_ref[0, rows, :], cos_ref[rows, :], sin_ref[rows, :], swap) * scale).astype(BF16)
        maps = []
        for m in range(2):
            qm = jnp.where(sub_head == m, qr, jnp.zeros_like(qr))
            sa = jnp.where(real_key, _dot_nt(qm, kr_ref[0:CHUNK, :]), -jnp.inf)
            sb = _dot_nt(qm, kr_ref[CHUNK:, :])
            mx = jnp.maximum(jnp.max(sa, axis=-1, keepdims=True), jnp.max(sb, axis=-1, keepdims=True))
            pa = jnp.exp2(sa - mx)
            pb = jnp.exp2(sb - mx)
            den = jnp.sum(pa, axis=-1, keepdims=True) + jnp.sum(pb, axis=-1, keepdims=True)
            pv = _dot(pa.astype(BF16), v_ref[0, 0:CHUNK, :]) + _dot(pb.astype(BF16), v_ref[0, CHUNK:, :])
            maps.append(pv * (1.0 / den))
        o = maps[0] - lam * maps[1]
        o_ref[0, rows, :] = (_rms(o, w_ref[...]) * (1.0 - lambda_init)).astype(o_ref.dtype)
        return carry

    lax.fori_loop(0, tp // tq, q_block, 0, unroll=unroll)


def diff_attention(qkv3, cos, sin, lam_params, subln_w, lambda_init, tq=272, unroll=2):
    b, tp, _ = qkv3.shape
    hw = 2 * DIFF_HEADDIM
    assert hw == LANES and tp % tq == 0 and tq % 16 == 0
    head = lambda k: pl.BlockSpec((1, tp, hw), lambda i, h: (i, 0, k * DIFF_HEADS + h))
    return pl.pallas_call(
        functools.partial(_diff_attn_kernel, lambda_init=lambda_init, tq=tq, unroll=unroll),
        grid=(b, DIFF_HEADS),
        in_specs=[head(0), head(1), head(2), _resident((tp, LANES)), _resident((tp, LANES)),
                  _resident(lam_params.shape), _resident((1, hw))],
        out_specs=head(0),
        out_shape=jax.ShapeDtypeStruct((b, tp, D_DIFF), BF16),
        scratch_shapes=[pltpu.VMEM((tp, hw), BF16)],
        compiler_params=_cparams(2), name="diff_attention")(
            qkv3, qkv3, qkv3, cos, sin, lam_params, subln_w.reshape(1, hw))


def _softplus(x):
    return jnp.maximum(x, 0.0) + jnp.log1p(jnp.exp(-jnp.abs(x)))


def _ssd_kernel(zx_ref, dt_ref, cw_ref, cb_ref, dtb_ref, alog_ref, aexp_ref, dsk_ref, nw_ref,
                o_ref, xact_ref, yacc_ref, dtv_ref, st_ref):
    tp = zx_ref.shape[1]
    n_chunks = tp // CHUNK
    gn = SSD_GROUPS * SSD_STATE

    valid = lax.broadcasted_iota(jnp.int32, (tp, LANES), 0) >= PAD
    dsum = dsk_ref[0:1, :] + dsk_ref[1:2, :]
    for c0 in range(0, SSD_CONV_DIM, LANES):
        x = jnp.where(valid, zx_ref[0, :, D_SSD + c0:D_SSD + c0 + LANES].astype(F32), 0.0)
        acc = cb_ref[:, c0:c0 + LANES] + x * cw_ref[SSD_CONV // 2:SSD_CONV // 2 + 1, c0:c0 + LANES]
        for k in range(SSD_CONV):
            off = k - SSD_CONV // 2
            if off != 0:
                acc = acc + pltpu.roll(x, (-off) % tp, axis=0) * cw_ref[k:k + 1, c0:c0 + LANES]
        act = _silu(acc)
        xact_ref[:, c0:c0 + LANES] = act.astype(BF16)
        if c0 < D_SSD:
            yacc_ref[:, c0:c0 + LANES] = act * dsum[:, c0:c0 + LANES]
    dtv_ref[...] = jnp.where(valid, _softplus(dt_ref[0] + dtb_ref[...]), 0.0)
    st_ref[...] = jnp.zeros_like(st_ref)

    ii = lax.broadcasted_iota(jnp.int32, (CHUNK, CHUNK), 0)
    jj = lax.broadcasted_iota(jnp.int32, (CHUNK, CHUNK), 1)
    lower = jj <= ii
    upper = jj >= ii
    tril = jnp.where(lower, 1.0, 0.0).astype(BF16)
    triu = jnp.where(upper, 1.0, 0.0).astype(BF16)
    a_row = -jnp.exp(alog_ref[...])
    src_lane = lax.broadcasted_iota(jnp.int32, (LANES, D_SSD), 0)
    dst_head = lax.broadcasted_iota(jnp.int32, (LANES, D_SSD), 1) // SSD_HEADDIM
    half = lax.broadcasted_iota(jnp.int32, (CHUNK, LANES), 1) // SSD_HEADDIM

    def chunk_step(c, d):
        rows = pl.ds(pl.multiple_of(c * CHUNK, CHUNK), CHUNK)
        tri_col, tri_row, mask = (tril, triu, lower) if d == 0 else (triu, tril, upper)
        expand = jnp.where(src_lane == dst_head + SSD_HEADS * d, 1.0, 0.0).astype(BF16)
        dtc = dtv_ref[rows, :]
        a = dtc * a_row
        dt_exp = _dot3_right(dtc, expand)
        a_exp = dt_exp * (-jnp.exp(aexp_ref[d:d + 1, :]))
        col = _dot3_left(tri_col, a_exp)
        rowv = _dot3_right(a.T, tri_row)
        tot = col[CHUNK - 1:CHUNK, :] if d == 0 else col[0:1, :]
        ecol = jnp.exp(col)
        wst = jnp.exp(tot - col)
        dec = jnp.exp(tot)
        xdt = xact_ref[rows, 0:D_SSD].astype(F32) * dt_exp
        xdt_b = xdt.astype(BF16)
        xw = (xdt * wst).astype(BF16)
        for g in range(SSD_GROUPS):
            bg = xact_ref[rows, D_SSD + g * SSD_STATE:D_SSD + (g + 1) * SSD_STATE]
            cg = xact_ref[rows, D_SSD + gn + g * SSD_STATE:D_SSD + gn + (g + 1) * SSD_STATE]
            cb = _dot_nt(cg, bg)
            bt = bg.astype(F32).T.astype(BF16)
            for pp in range(2):
                p = 2 * g + pp
                lanes = slice(p * LANES, (p + 1) * LANES)
                s_prev = st_ref[d, p]
                y = _dot(cg, s_prev.astype(BF16)) * ecol[:, lanes]
                for hh in range(2):
                    h = 2 * p + hh
                    ccol = col[:, h * SSD_HEADDIM:h * SSD_HEADDIM + 1]
                    crow = rowv[h + SSD_HEADS * d:h + SSD_HEADS * d + 1, :]
                    lmat = jnp.exp(jnp.where(mask, ccol - crow, -jnp.inf))
                    rhs = jnp.where(half == hh, xdt_b[:, lanes], jnp.zeros((CHUNK, LANES), BF16))
                    y = y + _dot((cb * lmat).astype(BF16), rhs)
                yacc_ref[rows, lanes] += y
                st_ref[d, p] = s_prev * dec[:, lanes] + _dot(bt, xw[:, lanes])

    def scan_body(i, carry):
        chunk_step(i, 0)
        chunk_step(n_chunks - 1 - i, 1)
        return carry

    lax.fori_loop(0, n_chunks, scan_body, 0)

    gw = D_SSD // SSD_GROUPS

    def gate_body(c, carry):
        rows = pl.ds(pl.multiple_of(c * CHUNK, CHUNK), CHUNK)
        yz = yacc_ref[rows, :] * _silu(zx_ref[0, rows, 0:D_SSD].astype(F32))
        for g in range(SSD_GROUPS):
            seg = _rms(yz[:, g * gw:(g + 1) * gw], nw_ref[:, g * gw:(g + 1) * gw])
            o_ref[0, rows, g * gw:(g + 1) * gw] = seg.astype(o_ref.dtype)
        return carry

    lax.fori_loop(0, n_chunks, gate_body, 0)


def ssd_mixer(zx3, dt3, conv_w, conv_b, a_log, dt_bias, d_skip, norm_w):
    b, tp, wzx = zx3.shape
    lane_pad = LANES - 2 * SSD_HEADS
    dtb = jnp.pad(dt_bias.reshape(1, -1), ((0, 0), (0, lane_pad)))
    alog = jnp.pad(a_log.reshape(1, -1), ((0, 0), (0, lane_pad)))
    aexp = jnp.repeat(a_log, SSD_HEADDIM, axis=1)
    dsk = jnp.repeat(d_skip, SSD_HEADDIM, axis=1)
    return pl.pallas_call(
        _ssd_kernel, grid=(b,),
        in_specs=[pl.BlockSpec((1, tp, wzx), lambda i: (i, 0, 0)),
                  pl.BlockSpec((1, tp, LANES), lambda i: (i, 0, 0)),
                  _resident(conv_w.shape), _resident((1, SSD_CONV_DIM)),
                  _resident((1, LANES)), _resident((1, LANES)),
                  _resident((2, D_SSD)), _resident((2, D_SSD)), _resident((1, D_SSD))],
        out_specs=pl.BlockSpec((1, tp, D_SSD), lambda i: (i, 0, 0)),
        out_shape=jax.ShapeDtypeStruct((b, tp, D_SSD), BF16),
        scratch_shapes=[pltpu.VMEM((tp, SSD_CONV_DIM), BF16), pltpu.VMEM((tp, D_SSD), F32),
                        pltpu.VMEM((tp, LANES), F32),
                        pltpu.VMEM((2, SSD_HEADS // 2, SSD_STATE, LANES), F32)],
        compiler_params=_cparams(1), name="ssd_mixer")(
            zx3, dt3, conv_w, conv_b.reshape(1, -1), dtb, alog, aexp, dsk, norm_w.reshape(1, -1))


def _hy_conv_kernel(p0_ref, p1_ref, p2_ref, w0_ref, w1_ref, w2_ref, b0_ref, b1_ref, b2_ref,
                    x0_ref, v_ref):
    tp = p0_ref.shape[1]
    ct = p0_ref.shape[2]
    valid = lax.broadcasted_iota(jnp.int32, (tp, LANES), 0) >= PAD

    def conv(p_ref, w_ref, b_ref, c0):
        x = jnp.where(valid, p_ref[0, :, c0:c0 + LANES].astype(F32), 0.0)
        acc = b_ref[:, c0:c0 + LANES] + x * w_ref[HYENA_SHORT // 2:HYENA_SHORT // 2 + 1, c0:c0 + LANES]
        for k in range(HYENA_SHORT):
            off = k - HYENA_SHORT // 2
            if off != 0:
                acc = acc + pltpu.roll(x, (-off) % tp, axis=0) * w_ref[k:k + 1, c0:c0 + LANES]
        return acc

    for c0 in range(0, ct, LANES):
        x0_ref[0, :, c0:c0 + LANES] = conv(p0_ref, w0_ref, b0_ref, c0).astype(x0_ref.dtype)
        v = conv(p2_ref, w2_ref, b2_ref, c0) * conv(p1_ref, w1_ref, b1_ref, c0)
        v_ref[0, :, c0:c0 + LANES] = jnp.where(valid, v, 0.0).astype(v_ref.dtype)


def hyena_conv_gate(p3, conv_w, conv_b, ct=256):
    b, tp, d3 = p3.shape
    d = d3 // 3
    nct = d // ct
    conv_b = conv_b.reshape(1, d3)
    pspec = lambda k: pl.BlockSpec((1, tp, ct), lambda i, j: (i, 0, j + k * nct))
    wspec = lambda k: pl.BlockSpec((HYENA_SHORT, ct), lambda i, j: (0, j + k * nct))
    bspec = lambda k: pl.BlockSpec((1, ct), lambda i, j: (0, j + k * nct))
    ospec = pl.BlockSpec((1, tp, ct), lambda i, j: (i, 0, j))
    return pl.pallas_call(
        _hy_conv_kernel, grid=(b, nct),
        in_specs=[pspec(0), pspec(1), pspec(2), wspec(0), wspec(1), wspec(2),
                  bspec(0), bspec(1), bspec(2)],
        out_specs=[ospec, ospec],
        out_shape=[jax.ShapeDtypeStruct((b, tp, d), BF16)] * 2,
        compiler_params=_cparams(2), name="hyena_conv_gate")(
            p3, p3, p3, conv_w, conv_w, conv_w, conv_b, conv_b, conv_b)


def _dot_f32(a, b):
    return jnp.dot(a, b, precision=lax.Precision.HIGHEST, preferred_element_type=F32)


def _hy_filter_kernel(w1t_ref, w1c_ref, w1s_ref, b1_ref, w2_ref, b2_ref, w3_ref, b3_ref, fr_ref,
                      w4f_ref, w4b_ref, bands_ref, deltas_ref, hf_ref, hb_ref, *, t_len):
    tp = hf_ref.shape[0]
    row = lax.broadcasted_iota(jnp.int32, (tp, 1), 0)
    pos = row.astype(F32)
    t = pos / (t_len - 1)
    ang = (2.0 * math.pi * pos / t_len) * bands_ref[...]
    h = t * w1t_ref[...] + _dot_f32(jnp.cos(ang), w1c_ref[...]) + _dot_f32(-jnp.sin(ang), w1s_ref[...])
    h = jnp.sin(fr_ref[0:1, :] * (h + b1_ref[...]))
    h = jnp.sin(fr_ref[1:2, :] * (_dot_f32(h, w2_ref[...]) + b2_ref[...]))
    h = jnp.sin(fr_ref[2:3, :] * (_dot_f32(h, w3_ref[...]) + b3_ref[...]))
    decay = jnp.exp(-t * deltas_ref[...]) + HYENA_SHIFT
    hf = jnp.where(row < t_len, _dot_f32(h, w4f_ref[...]) * decay, 0.0)
    hb = jnp.where((row >= 1) & (row < t_len), _dot_f32(h, w4b_ref[...]) * decay, 0.0)
    norm = (jnp.sum(jnp.abs(hf), axis=0, keepdims=True)
            + jnp.sum(jnp.abs(hb), axis=0, keepdims=True))
    hf_ref[...] = hf / norm
    hb_ref[...] = hb / norm


def hyena_filters(tp, t_len, w1, b1, w2, b2, w3, b3, w4, freq, ct=256):
    d = w4.shape[1] // 2
    fw = w1.shape[1]
    nct = d // ct
    bands = jnp.linspace(1e-4, HYENA_BANDS - 1, HYENA_BANDS, dtype=F32).reshape(1, -1)
    max_decay = math.log(HYENA_TARGET) / HYENA_FAST_DECAY
    min_decay = math.log(HYENA_TARGET) / HYENA_SLOW_DECAY
    deltas = jnp.abs(jnp.linspace(min_decay, max_decay, d, dtype=F32)).reshape(1, -1)
    nb = HYENA_BANDS
    full = lambda a: pl.BlockSpec(a.shape, lambda j: (0,) * a.ndim)
    args = [w1[0:1], w1[1:1 + nb], w1[1 + nb:], b1.reshape(1, fw), w2, b2.reshape(1, fw),
            w3, b3.reshape(1, fw), freq]
    return pl.pallas_call(
        functools.partial(_hy_filter_kernel, t_len=t_len), grid=(nct,),
        in_specs=[full(a) for a in args] + [
            pl.BlockSpec((fw, ct), lambda j: (0, j)), pl.BlockSpec((fw, ct), lambda j: (0, j + nct)),
            full(bands), pl.BlockSpec((1, ct), lambda j: (0, j))],
        out_specs=[pl.BlockSpec((tp, ct), lambda j: (0, j))] * 2,
        out_shape=[jax.ShapeDtypeStruct((tp, d), F32)] * 2,
        compiler_params=_cparams(1), name="hyena_filters")(*args, w4, w4, bands, deltas)


def _dft_table_kernel(cs_ref, *, period):
    tr = cs_ref.shape[0]
    tp = cs_ref.shape[1] // 2
    a = lax.broadcasted_iota(jnp.int32, (tr, tp), 0) + pl.program_id(0) * tr
    b = lax.broadcasted_iota(jnp.int32, (tr, tp), 1)
    n = (2 * a + 1) * (2 * b + 1)
    q = jnp.floor(n.astype(F32) * (1.0 / period)).astype(jnp.int32)
    r = (n - q * period).astype(F32)
    ang = r * (2.0 * math.pi / period)
    cs_ref[:, :tp] = jnp.cos(ang).astype(cs_ref.dtype)
    cs_ref[:, tp:] = jnp.sin(ang).astype(cs_ref.dtype)


def dft_tables(tp, tr=272):
    return pl.pallas_call(
        functools.partial(_dft_table_kernel, period=8 * tp), grid=(tp // tr,),
        out_specs=pl.BlockSpec((tr, 2 * tp), lambda i: (i, 0)),
        out_shape=jax.ShapeDtypeStruct((tp, 2 * tp), BF16),
        compiler_params=_cparams(1), name="dft_tables")()


def _split2(x):
    hi = x.astype(BF16)
    return hi, (x - hi.astype(F32)).astype(BF16)


def _hy_spectrum_kernel(hf_ref, hb_ref, cs_ref, p_ref, q_ref, *, mc):
    tp = hf_ref.shape[0]
    big_l = 2 * tp
    scale = 2.0 / big_l
    hs_hi, hs_lo = _split2(hf_ref[...] + hb_ref[...])
    hd_hi, hd_lo = _split2(hf_ref[...] - hb_ref[...])
    for m0 in range(0, tp, mc):
        rows = slice(m0, m0 + mc)
        phi = ((lax.broadcasted_iota(jnp.int32, (mc, 1), 0) + m0).astype(F32) + 0.5) * (math.pi / big_l)
        cphi, sphi = jnp.cos(phi), jnp.sin(phi)
        cm, sm = cs_ref[rows, :tp], cs_ref[rows, tp:]
        c_hs = _dot(cm, hs_hi) + _dot(cm, hs_lo)
        s_hs = _dot(sm, hs_hi) + _dot(sm, hs_lo)
        c_hd = _dot(cm, hd_hi) + _dot(cm, hd_lo)
        s_hd = _dot(sm, hd_hi) + _dot(sm, hd_lo)
        p_ref[rows, :] = (c_hs * cphi + s_hs * sphi) * scale
        q_ref[rows, :] = (s_hd * cphi - c_hd * sphi) * scale


def hyena_spectrum(hf, hb, cs, ct=256, mc=272):
    tp, d = hf.shape
    cspec = pl.BlockSpec((tp, ct), lambda j: (0, j))
    return pl.pallas_call(
        functools.partial(_hy_spectrum_kernel, mc=mc), grid=(d // ct,),
        in_specs=[cspec, cspec, _resident(cs.shape)],
        out_specs=[cspec, cspec],
        out_shape=[jax.ShapeDtypeStruct((tp, d), F32)] * 2,
        compiler_params=_cparams(1), name="hyena_spectrum")(hf, hb, cs)


def _hy_longconv_kernel(v_ref, x0_ref, p_ref, q_ref, skip_ref, cs_ref, o_ref, ycs_ref, *, mc):
    tp = v_ref.shape[1]
    v = v_ref[0]
    for m0 in range(0, tp, mc):
        rows = slice(m0, m0 + mc)
        a = _dot(cs_ref[rows, :tp], v)
        b = _dot(cs_ref[rows, tp:], v)
        p, q = p_ref[rows, :], q_ref[rows, :]
        ycs_ref[m0:m0 + mc, :] = (a * p - b * q).astype(BF16)
        ycs_ref[tp + m0:tp + m0 + mc, :] = (a * q + b * p).astype(BF16)
    for m0 in range(0, tp, mc):
        rows = slice(m0, m0 + mc)
        y = _dot(cs_ref[rows, :], ycs_ref[...])
        y = (y + v_ref[0, rows, :].astype(F32) * skip_ref[...]) * x0_ref[0, rows, :].astype(F32)
        o_ref[0, rows, :] = y.astype(o_ref.dtype)


def hyena_longconv(v3, x03, p, q, skip, cs, ct=256, mc=272):
    b, tp, d = v3.shape
    bspec = pl.BlockSpec((1, tp, ct), lambda j, i: (i, 0, j))
    cspec = pl.BlockSpec((tp, ct), lambda j, i: (0, j))
    return pl.pallas_call(
        functools.partial(_hy_longconv_kernel, mc=mc), grid=(d // ct, b),
        in_specs=[bspec, bspec, cspec, cspec, pl.BlockSpec((1, ct), lambda j, i: (0, j)),
                  _resident(cs.shape)],
        out_specs=bspec,
        out_shape=jax.ShapeDtypeStruct((b, tp, d), BF16),
        scratch_shapes=[pltpu.VMEM((2 * tp, ct), BF16)],
        compiler_params=_cparams(2), name="hyena_longconv")(
            v3, x03, p, q, skip.reshape(1, d), cs)


HI_MASK = 0xFFFF0000


def _pack_halves(x):
    w = x.shape[1] // 2
    lo = pltpu.bitcast(x[:, :w].astype(BF16).astype(F32), jnp.uint32)
    hi = pltpu.bitcast(x[:, w:].astype(BF16).astype(F32), jnp.uint32)
    return (lo >> 16) | (hi & jnp.uint32(HI_MASK))


def _unpack_halves(w):
    return (pltpu.bitcast(w << 16, F32), pltpu.bitcast(w & jnp.uint32(HI_MASK), F32))


def _router_kernel(x_ref, a_ref, wo_ref, bo_ref, g_ref, wr_ref, valid_ref,
                   h_ref, xp_ref, gate_ref, kind_ref, cnt_ref, carry_ref):
    @pl.when(pl.program_id(0) == 0)
    def _():
        carry_ref[...] = jnp.zeros_like(carry_ref)

    h = x_ref[...] + _dot(a_ref[...], wo_ref[...]) + bo_ref[...]
    h_ref[...] = h
    xn = _rms(h, g_ref[...])
    xp_ref[...] = _pack_halves(xn)
    tm = xn.shape[0]
    lane = lax.broadcasted_iota(jnp.int32, (tm, LANES), 1)
    logits = jnp.where(lane < N_EXPERTS, _dot_f32(xn, wr_ref[...]), -jnp.inf)
    v1 = jnp.max(logits, axis=-1, keepdims=True)
    i1 = jnp.min(jnp.where(logits == v1, lane, LANES), axis=-1, keepdims=True)
    rest = jnp.where(lane == i1, -jnp.inf, logits)
    v2 = jnp.max(rest, axis=-1, keepdims=True)
    i2 = jnp.min(jnp.where(rest == v2, lane, LANES), axis=-1, keepdims=True)
    e2 = jnp.exp(v2 - v1)
    den = 1.0 + e2
    gates = jnp.where(lane == i1, 1.0 / den, 0.0) + jnp.where(lane == i2, e2 / den, 0.0)
    kind = jnp.where(lane == i1, 1.0, 0.0) + jnp.where(lane == i2, 2.0, 0.0)
    valid = valid_ref[...]
    gate_t = gates.T[0:N_EXPERTS, :] * valid
    kind_t = kind.T[0:N_EXPERTS, :] * valid
    gate_ref[...] = gate_t
    kind_ref[...] = kind_t
    member = jnp.where(kind_t > 0.0, 1.0, 0.0)
    earlier = (lax.broadcasted_iota(jnp.int32, (tm, tm), 0)
               <= lax.broadcasted_iota(jnp.int32, (tm, tm), 1))
    running = _dot(member.astype(BF16), jnp.where(earlier, 1.0, 0.0).astype(BF16))
    cnt_ref[...] = running + carry_ref[:, 0:1]
    carry_ref[...] = carry_ref[...] + jnp.sum(member, axis=-1, keepdims=True)


def route_top2(x, a, wo, bo, g, router, tp, tm=512):
    n_rows, d = x.shape
    tm = _row_tile(n_rows, tm)
    wr = jnp.pad(router, ((0, 0), (0, LANES - router.shape[1])))
    valid = ((jnp.arange(n_rows, dtype=jnp.int32) % tp) >= PAD).astype(F32).reshape(1, n_rows)
    row = lambda w: pl.BlockSpec((tm, w), lambda i: (i, 0))
    espec = pl.BlockSpec((N_EXPERTS, tm), lambda i: (0, i))
    eshape = jax.ShapeDtypeStruct((N_EXPERTS, n_rows), F32)
    return pl.pallas_call(
        _router_kernel, grid=(n_rows // tm,),
        in_specs=[row(d), row(a.shape[1]), _resident(wo.shape), _resident((1, d)),
                  _resident((1, d)), _resident(wr.shape), pl.BlockSpec((1, tm), lambda i: (0, i))],
        out_specs=[row(d), row(d // 2), espec, espec, espec],
        out_shape=[jax.ShapeDtypeStruct((n_rows, d), F32),
                   jax.ShapeDtypeStruct((n_rows, d // 2), jnp.uint32), eshape, eshape, eshape],
        scratch_shapes=[pltpu.VMEM((N_EXPERTS, LANES), F32)],
        compiler_params=_cparams(1), name="moe_router")(
            x, a, wo, bo.reshape(1, d), g.reshape(1, d), wr, valid)


def _route_invert_kernel(te_ref, q0_ref, lo_ref, hi_ref, cnt_ref, gate_ref, kind_ref, trash_ref,
                         tok_ref, slot_ref, gcol_ref, cols_ref, *, n_rows, n_win):
    j = pl.program_id(0)
    e = te_ref[j]
    tm = tok_ref.shape[2]
    out_lane = lax.broadcasted_iota(jnp.int32, (SUB_ROWS, LANES), 1)
    total = cnt_ref[e, :, n_rows - 1:n_rows]
    for s in range(tm // SUB_ROWS):
        first = q0_ref[j] + s * SUB_ROWS + 1
        want = (first + lax.broadcasted_iota(jnp.int32, (SUB_ROWS, 1), 0)).astype(F32)
        lo = lo_ref[j * (tm // SUB_ROWS) + s]

        def block(k, acc, want=want):
            below, weight, second = acc
            cols = pl.ds(pl.multiple_of(k * LANES, LANES), LANES)
            c = cnt_ref[e, :, cols]
            kd = kind_ref[e, :, cols]
            below = below + jnp.where(c < want, 1.0, 0.0)
            hit = jnp.where(kd > 0.0, c, -1.0) == want
            weight = weight + jnp.where(hit, gate_ref[e, :, cols], 0.0)
            second = second + jnp.where(hit, jnp.where(kd == 2.0, 1.0, 0.0), 0.0)
            return below, weight, second

        zero = jnp.zeros((SUB_ROWS, LANES), F32)
        below, weight, second = lax.fori_loop(lo, hi_ref[j * (tm // SUB_ROWS) + s], block,
                                              (zero, zero, zero))
        rows = slice(s * SUB_ROWS, (s + 1) * SUB_ROWS)
        gcol_ref[rows, :] = jnp.sum(weight, axis=-1, keepdims=True)
        tok_col = (lo * LANES).astype(F32) + jnp.sum(below, axis=-1, keepdims=True)
        cols_ref[rows, :] = jnp.where(out_lane == 0, tok_col, jnp.where(
            out_lane == 1, jnp.sum(second, axis=-1, keepdims=True), jnp.where(
                out_lane == 2, jnp.where(want <= total, 1.0, 0.0), 0.0)))
    t = cols_ref[...].T
    found = t[2:3, :] > 0.5
    tok = jnp.where(found, t[0:1, :], 0.0).astype(jnp.int32)
    plane = jnp.where(t[1:2, :] > 0.5, n_rows, 0)
    trash = trash_ref[:, pl.ds(pl.multiple_of((j % n_win) * tm, tm), tm)]
    tok_ref[0] = tok
    slot_ref[0] = jnp.where(found, plane + tok, trash)


def route_plan(gate_t, kind_t, cnt_t, b, tp, tm):
    n = b * tp
    n_tiles = (TOP_K * b * (tp - PAD)) // tm + N_EXPERTS
    n_blk = n // LANES
    counts = cnt_t[:, -1].astype(jnp.int32)
    tiles = (counts + tm - 1) // tm
    tile_end = jnp.cumsum(tiles)
    tile_id = jnp.arange(n_tiles, dtype=jnp.int32)
    te = jnp.minimum(jnp.sum(tile_id[:, None] >= tile_end[None, :], axis=1), N_EXPERTS - 1)
    te = te.astype(jnp.int32)
    q0 = (tile_id - (tile_end - tiles)[te]) * tm
    n_sub = tm // SUB_ROWS
    block_end = cnt_t[:, LANES - 1::LANES].astype(jnp.int32)[te]
    first = q0[:, None] + jnp.arange(n_sub, dtype=jnp.int32)[None, :] * SUB_ROWS + 1
    last = jnp.minimum(first + SUB_ROWS - 1, counts[te][:, None])
    lo = jnp.sum(block_end[:, None, :] < first[:, :, None], axis=2)
    hi = jnp.minimum(jnp.sum(block_end[:, None, :] < last[:, :, None], axis=2) + 1, n_blk)
    hi = jnp.maximum(hi, lo).reshape(-1)
    lo = lo.reshape(-1)
    n_pad_rows = TOP_K * b * PAD
    n_win = max(1, n_pad_rows // tm)
    p = jnp.arange(n_win * tm, dtype=jnp.int32) % n_pad_rows
    trash = ((p // (b * PAD)) * n + ((p % (b * PAD)) // PAD) * tp + p % PAD).reshape(1, n_win * tm)
    grid_spec = pltpu.PrefetchScalarGridSpec(
        num_scalar_prefetch=4, grid=(n_tiles,),
        in_specs=[_resident((N_EXPERTS, 1, n))] * 3 + [_resident(trash.shape)],
        out_specs=[pl.BlockSpec((1, 1, tm), lambda j, *_: (j, 0, 0)),
                   pl.BlockSpec((1, 1, tm), lambda j, *_: (j, 0, 0)),
                   pl.BlockSpec((tm, 1), lambda j, *_: (j, 0))],
        scratch_shapes=[pltpu.VMEM((tm, LANES), F32)])
    tok, slot, gate = pl.pallas_call(
        functools.partial(_route_invert_kernel, n_rows=n, n_win=n_win), grid_spec=grid_spec,
        out_shape=[jax.ShapeDtypeStruct((n_tiles, 1, tm), jnp.int32),
                   jax.ShapeDtypeStruct((n_tiles, 1, tm), jnp.int32),
                   jax.ShapeDtypeStruct((n_tiles * tm, 1), F32)],
        compiler_params=_cparams(1), name="route_invert")(
            te, q0.astype(jnp.int32), lo.astype(jnp.int32), hi.astype(jnp.int32),
            cnt_t.reshape(N_EXPERTS, 1, n), gate_t.reshape(N_EXPERTS, 1, n),
            kind_t.reshape(N_EXPERTS, 1, n), trash)
    trash_tile = trash[:, (n_win - 1) * tm:].reshape(1, 1, tm)
    return tok, slot, gate, te, trash_tile


def _moe_sparse_kernel(te_ref, tokc_ref, tok1_ref, slotp_ref, slotc_ref, gate_ref, xp_hbm,
                       wg_ref, wu_ref, wd_ref, out_hbm, gbuf, ybuf, hm_ref, zbuf, gsem, ssem, zsem,
                       *, tf, tp):
    j = pl.program_id(0)
    tm = gbuf.shape[1]
    cur = j % 2

    @pl.when(j == 0)
    def _():
        zbuf[...] = jnp.zeros_like(zbuf)
        n_seq = out_hbm.shape[0] // tp
        copies = [pltpu.make_async_copy(zbuf, out_hbm.at[pl.ds(s * tp, PAD)], zsem)
                  for s in range(n_seq)]
        for c in copies:
            c.start()
        for c in copies:
            c.wait()

    prev = 1 - cur
    last = pl.num_programs(0) - 1

    def gather_row(tok_ref, buf_slot, r):
        return pltpu.make_async_copy(xp_hbm.at[pl.ds(tok_ref[0, 0, r], 1)],
                                     gbuf.at[buf_slot, pl.ds(r, 1)], gsem.at[buf_slot])

    def scatter_row(dst_ref, buf_slot, r):
        return pltpu.make_async_copy(ybuf.at[buf_slot, pl.ds(r, 1)],
                                     out_hbm.at[pl.ds(dst_ref[0, 0, r], 1)], ssem.at[buf_slot])

    def wait_gather(buf_slot):
        pltpu.make_async_copy(xp_hbm.at[pl.ds(0, tm)], gbuf.at[buf_slot], gsem.at[buf_slot]).wait()

    def wait_scatter(buf_slot):
        pltpu.make_async_copy(ybuf.at[buf_slot], out_hbm.at[pl.ds(0, tm)], ssem.at[buf_slot]).wait()

    @pl.when(j == 0)
    def _():
        ybuf[1] = jnp.zeros((tm, ybuf.shape[2]), ybuf.dtype)

        def body(r, carry):
            gather_row(tokc_ref, 0, r).start()
            return carry
        lax.fori_loop(0, tm, body, 0, unroll=8)

    wait_gather(cur)
    lo, hi = _unpack_halves(gbuf[cur])
    x = jnp.concatenate([lo.astype(BF16), hi.astype(BF16)], axis=1)
    gate = gate_ref[...]
    f = wg_ref.shape[2]
    n_chunks = f // tf
    moves = ([functools.partial(gather_row, tok1_ref, prev, r) for r in range(tm)]
             + [functools.partial(scatter_row, slotp_ref, prev, r) for r in range(tm)])
    per_chunk = -(-len(moves) // n_chunks)
    for ci in range(n_chunks):
        c0 = ci * tf
        gt = _dot(x, wg_ref[0, :, c0:c0 + tf])
        ut = _dot(x, wu_ref[0, :, c0:c0 + tf])
        hm_ref[:, c0:c0 + tf] = (_silu(gt) * ut * gate).astype(BF16)
        for move in moves[ci * per_chunk:(ci + 1) * per_chunk]:
            move().start()
    y = _pack_halves(_dot(hm_ref[...], wd_ref[0]))

    @pl.when(j >= 1)
    def _():
        wait_scatter(cur)

    ybuf[cur] = y

    @pl.when(j == last)
    def _():
        def body(r, carry):
            scatter_row(slotc_ref, cur, r).start()
            return carry
        lax.fori_loop(0, tm, body, 0, unroll=8)
        wait_scatter(prev)
        wait_scatter(cur)
        wait_gather(prev)


def moe_sparse(xp, plan, wg, wu, wd, tp, tf=512):
    tok, slot, gate, tile_expert, trash_tile = plan
    n_tiles, _, tm = tok.shape
    n_rows, dh = xp.shape
    n_e, d, fe = wg.shape
    slot = jnp.concatenate([trash_tile, slot])
    smem_tile = lambda imap: pl.BlockSpec((1, 1, tm), imap, memory_space=pltpu.SMEM)
    wspec = lambda shape: pl.BlockSpec(shape, lambda j, te: (te[j], 0, 0),
                                       pipeline_mode=pl.Buffered(1))
    grid_spec = pltpu.PrefetchScalarGridSpec(
        num_scalar_prefetch=1, grid=(n_tiles,),
        in_specs=[smem_tile(lambda j, te: (j, 0, 0)),
                  smem_tile(lambda j, te: (jnp.minimum(j + 1, n_tiles - 1), 0, 0)),
                  smem_tile(lambda j, te: (j, 0, 0)),
                  smem_tile(lambda j, te: (j + 1, 0, 0)),
                  pl.BlockSpec((tm, 1), lambda j, te: (j, 0)),
                  pl.BlockSpec(memory_space=pl.ANY),
                  wspec((1, d, fe)), wspec((1, d, fe)), wspec((1, fe, d))],
        out_specs=pl.BlockSpec(memory_space=pl.ANY),
        scratch_shapes=[pltpu.VMEM((2, tm, dh), jnp.uint32), pltpu.VMEM((2, tm, dh), jnp.uint32),
                        pltpu.VMEM((tm, fe), BF16), pltpu.VMEM((PAD, dh), jnp.uint32),
                        pltpu.SemaphoreType.DMA((2,)), pltpu.SemaphoreType.DMA((2,)),
                        pltpu.SemaphoreType.DMA(())])
    return pl.pallas_call(
        functools.partial(_moe_sparse_kernel, tf=tf, tp=tp), grid_spec=grid_spec,
        out_shape=jax.ShapeDtypeStruct((TOP_K * n_rows, dh), jnp.uint32),
        compiler_params=_cparams(1), name="moe_sparse")(
            tile_expert, tok, tok, slot, slot, gate, xp, wg, wu, wd)


def _moe_combine_norm_kernel(h_ref, y0_ref, y1_ref, g_ref, o_ref):
    dh = y0_ref.shape[3]
    a_lo, a_hi = _unpack_halves(y0_ref[0, 0])
    b_lo, b_hi = _unpack_halves(y1_ref[0, 0])
    lo = h_ref[0, :, :dh] + a_lo + b_lo
    hi = h_ref[0, :, dh:] + a_hi + b_hi
    ms = (jnp.sum(lo * lo, axis=-1, keepdims=True)
          + jnp.sum(hi * hi, axis=-1, keepdims=True)) / (2 * dh)
    inv = lax.rsqrt(ms + EPS)
    o_ref[0, :, :dh] = lo * inv * g_ref[:, :dh]
    o_ref[0, :, dh:] = hi * inv * g_ref[:, dh:]


def moe_combine_final_norm(h3, y2, g, seq):
    b, tp, d = h3.shape
    assert seq % CHUNK == 0 and tp == seq + CHUNK
    y4 = y2.reshape(TOP_K, b, tp, d // 2)
    yspec = lambda k: pl.BlockSpec((1, 1, CHUNK, d // 2), lambda i, j: (k, i, j + 1, 0))
    return pl.pallas_call(
        _moe_combine_norm_kernel, grid=(b, seq // CHUNK),
        in_specs=[pl.BlockSpec((1, CHUNK, d), lambda i, j: (i, j + 1, 0)), yspec(0), yspec(1),
                  _resident((1, d))],
        out_specs=pl.BlockSpec((1, CHUNK, d), lambda i, j: (i, j, 0)),
        out_shape=jax.ShapeDtypeStruct((b, seq, d), F32),
        compiler_params=_cparams(2), name="moe_combine_norm")(h3, y4, y4, g.reshape(1, d))


def kernel(x, meta_tokens, norm_mix_even, w_in_ab, ssd_conv_w, ssd_conv_b, ssd_a_log,
           ssd_dt_bias, ssd_d, ssd_norm_w, diff_lambda, diff_subln_w, w_out_ab,
           norm_ffn_even, ffn_w_gate, ffn_w_up, ffn_w_down, norm_mix_odd, hy_w_in, hy_b_in,
           hy_conv_w, hy_conv_b, hy_f_w1, hy_f_b1, hy_f_w2, hy_f_b2, hy_f_w3, hy_f_b3,
           hy_f_w4, hy_f_freq, hy_skip, hy_w_out, hy_b_out, norm_ffn_odd, moe_router,
           moe_w_gate, moe_w_up, moe_w_down, final_norm):
    b, seq, d = x.shape
    assert d == D_MODEL and seq % 512 == 0
    tp = seq + CHUNK
    t_len = seq + N_META
    n = b * tp
    meta = jnp.broadcast_to(meta_tokens[None].astype(x.dtype), (b, N_META, d))
    h = jnp.concatenate([jnp.zeros((b, PAD, d), x.dtype), meta, x], axis=1).reshape(n, d)
    cos, sin = rope_tables(tp)
    depth = norm_mix_even.shape[0] + norm_mix_odd.shape[0]
    o1 = D_SSD + SSD_CONV_DIM
    o2 = o1 + 2 * SSD_HEADS
    for layer in range(depth):
        i = layer // 2
        if layer % 2 == 0:
            w = w_in_ab[i]
            w_dt = jnp.pad(w[:, o1:o2], ((0, 0), (0, LANES - 2 * SSD_HEADS)))
            zx, dt, qkv = rms_proj(
                h, norm_mix_even[i],
                [w[:, :o1].astype(BF16), w_dt.astype(BF16), w[:, o2:].astype(BF16)],
                None, [BF16, F32, BF16])
            ssd_out = ssd_mixer(zx.reshape(b, tp, -1), dt.reshape(b, tp, -1), ssd_conv_w[i],
                                ssd_conv_b[i], ssd_a_log[i], ssd_dt_bias[i], ssd_d[i],
                                ssd_norm_w[i])
            lambda_init = 0.8 - 0.6 * math.exp(-0.3 * layer)
            attn_out = diff_attention(qkv.reshape(b, tp, -1), cos, sin, diff_lambda[i],
                                      diff_subln_w[i], lambda_init)
            wo = w_out_ab[i].astype(BF16)
            h = ffn_residual(h, [ssd_out.reshape(n, -1), attn_out.reshape(n, -1)],
                             [wo[:D_SSD], wo[D_SSD:]], norm_ffn_even[i],
                             ffn_w_gate[i].astype(BF16), ffn_w_up[i].astype(BF16),
                             ffn_w_down[i].astype(BF16))
        else:
            (p,) = rms_proj(h, norm_mix_odd[i], [hy_w_in[i].astype(BF16)],
                            [hy_b_in[i].reshape(1, -1)], [BF16])
            x0, v = hyena_conv_gate(p.reshape(b, tp, -1), hy_conv_w[i], hy_conv_b[i])
            hf, hb = hyena_filters(tp, t_len, hy_f_w1[i], hy_f_b1[i], hy_f_w2[i], hy_f_b2[i],
                                   hy_f_w3[i], hy_f_b3[i], hy_f_w4[i], hy_f_freq[i])
            cs = dft_tables(tp)
            fp, fq = hyena_spectrum(hf, hb, cs)
            y = hyena_longconv(v, x0, fp, fq, hy_skip[i], cs)
            assert layer == depth - 1, "an expert layer must close the trunk"
            h, xp, gate_t, kind_t, cnt_t = route_top2(
                h, y.reshape(n, -1), hy_w_out[i].astype(BF16), hy_b_out[i], norm_ffn_odd[i],
                moe_router[i], tp)
            plan = route_plan(gate_t, kind_t, cnt_t, b, tp, MOE_TILE)
            y2 = moe_sparse(xp, plan, moe_w_gate[i].astype(BF16), moe_w_up[i].astype(BF16),
                            moe_w_down[i].astype(BF16), tp)
            return moe_combine_final_norm(h.reshape(b, tp, d), y2, final_norm, seq)
    return final_rmsnorm(h.reshape(b, tp, d), final_norm, seq)
```

```python
import functools
import math

import jax
import jax.numpy as jnp
from jax import lax
from jax.experimental import pallas as pl
from jax.experimental.pallas import tpu as pltpu

F32 = jnp.float32
BF16 = jnp.bfloat16

D_MODEL = 1024
N_META = 16
EPS = 1e-5
D_SSD = 512
SSD_HEADDIM = 64
SSD_HEADS = 8
SSD_GROUPS = 2
SSD_STATE = 128
SSD_CONV = 5
CHUNK = 128
SSD_CONV_DIM = D_SSD + 2 * SSD_GROUPS * SSD_STATE
D_DIFF = 512
DIFF_HEADDIM = 64
DIFF_HEADS = 4
ROPE_THETA = 500000.0
ROPE_DIM = 16
HYENA_SHORT = 3
HYENA_BANDS = 16
HYENA_SHIFT = 0.05
HYENA_TARGET = 1e-2
HYENA_FAST_DECAY = 0.3
HYENA_SLOW_DECAY = 1.5
N_EXPERTS = 8
TOP_K = 2

LANES = 128
PAD = CHUNK - N_META
VMEM_LIMIT = 48 * 1024 * 1024
MOE_TILE = 512
SUB_ROWS = 64


def _cparams(n_axes):
    return pltpu.CompilerParams(dimension_semantics=("arbitrary",) * n_axes,
                                vmem_limit_bytes=VMEM_LIMIT)


def _resident(shape):
    zeros = (0,) * len(shape)
    return pl.BlockSpec(shape, lambda *_: zeros, pipeline_mode=pl.Buffered(1))


def _row_tile(n_rows, preferred):
    assert n_rows % CHUNK == 0
    tm = preferred
    while n_rows % tm:
        tm //= 2
    return tm


def _dot(a, b):
    return jnp.dot(a, b, preferred_element_type=F32)


def _dot_nt(a, b):
    return lax.dot_general(a, b, (((1,), (1,)), ((), ())), preferred_element_type=F32)


def _split3(x):
    hi = x.astype(BF16)
    r = x - hi.astype(F32)
    mid = r.astype(BF16)
    lo = (r - mid.astype(F32)).astype(BF16)
    return hi, mid, lo


def _dot3_right(x, m):
    hi, mid, lo = _split3(x)
    return _dot(hi, m) + _dot(mid, m) + _dot(lo, m)


def _dot3_left(m, x):
    hi, mid, lo = _split3(x)
    return _dot(m, hi) + _dot(m, mid) + _dot(m, lo)


def _rms(x, w):
    return x * lax.rsqrt(jnp.mean(x * x, axis=-1, keepdims=True) + EPS) * w


def _silu(x):
    return x * jax.nn.sigmoid(x)


def _rms_proj_kernel(*refs, n_out, has_bias, col_chunk):
    x_ref, g_ref = refs[:2]
    w_refs = refs[2:2 + n_out]
    b_refs = refs[2 + n_out:2 + 2 * n_out] if has_bias else (None,) * n_out
    o_refs = refs[-n_out:]
    xn = _rms(x_ref[...], g_ref[...]).astype(BF16)
    for w_ref, b_ref, o_ref in zip(w_refs, b_refs, o_refs):
        n = o_ref.shape[-1]
        for c0 in range(0, n, col_chunk):
            c1 = min(n, c0 + col_chunk)
            y = _dot(xn, w_ref[:, c0:c1])
            if b_ref is not None:
                y = y + b_ref[:, c0:c1]
            o_ref[:, c0:c1] = y.astype(o_ref.dtype)


def rms_proj(x, g, ws, bs, out_dtypes, tm=1024):
    n_rows, d = x.shape
    tm = _row_tile(n_rows, tm)
    n_out = len(ws)
    has_bias = bs is not None
    in_specs = [pl.BlockSpec((tm, d), lambda i: (i, 0)), _resident((1, d))]
    in_specs += [_resident(w.shape) for w in ws]
    args = [x, g.reshape(1, d)] + list(ws)
    if has_bias:
        in_specs += [_resident(b.shape) for b in bs]
        args += list(bs)
    out_shape = [jax.ShapeDtypeStruct((n_rows, w.shape[1]), dt) for w, dt in zip(ws, out_dtypes)]
    out_specs = [pl.BlockSpec((tm, w.shape[1]), lambda i: (i, 0)) for w in ws]
    return pl.pallas_call(
        functools.partial(_rms_proj_kernel, n_out=n_out, has_bias=has_bias, col_chunk=512),
        grid=(n_rows // tm,), in_specs=in_specs, out_specs=out_specs, out_shape=out_shape,
        compiler_params=_cparams(1), name="rms_proj")(*args)


def _ffn_kernel(*refs, n_in, tf):
    x_ref = refs[0]
    a_refs = refs[1:1 + n_in]
    wo_refs = refs[1 + n_in:1 + 2 * n_in]
    g_ref, wg_ref, wu_ref, wd_ref, o_ref, hm_ref = refs[1 + 2 * n_in:]
    x = x_ref[...]
    for a_ref, wo_ref in zip(a_refs, wo_refs):
        x = x + _dot(a_ref[...], wo_ref[...])
    xn = _rms(x, g_ref[...]).astype(BF16)
    f = wg_ref.shape[1]
    for c0 in range(0, f, tf):
        gt = _dot(xn, wg_ref[:, c0:c0 + tf])
        ut = _dot(xn, wu_ref[:, c0:c0 + tf])
        hm_ref[:, c0:c0 + tf] = (_silu(gt) * ut).astype(BF16)
    o_ref[...] = x + _dot(hm_ref[...], wd_ref[...])


def ffn_residual(x, a_list, wo_list, g, wg, wu, wd, tm=512, tf=256):
    n_rows, d = x.shape
    tm = _row_tile(n_rows, tm)
    f = wg.shape[1]
    row = lambda w: pl.BlockSpec((tm, w), lambda i: (i, 0))
    return pl.pallas_call(
        functools.partial(_ffn_kernel, n_in=len(a_list), tf=tf),
        grid=(n_rows // tm,),
        in_specs=[row(d)] + [row(a.shape[1]) for a in a_list] + [_resident(w.shape) for w in wo_list]
        + [_resident((1, d)), _resident(wg.shape), _resident(wu.shape), _resident(wd.shape)],
        out_specs=row(d),
        out_shape=jax.ShapeDtypeStruct((n_rows, d), F32),
        scratch_shapes=[pltpu.VMEM((tm, f), BF16)],
        compiler_params=_cparams(1), name="ffn")(
            x, *a_list, *wo_list, g.reshape(1, d), wg, wu, wd)


def _final_norm_kernel(x_ref, g_ref, o_ref):
    seq = o_ref.shape[1]
    for r0 in range(0, seq, 512):
        o_ref[0, r0:r0 + 512, :] = _rms(x_ref[0, CHUNK + r0:CHUNK + r0 + 512, :], g_ref[...])


def final_rmsnorm(h3, g, seq):
    b, tp, d = h3.shape
    assert seq % 512 == 0 and tp == seq + CHUNK
    return pl.pallas_call(
        _final_norm_kernel, grid=(b,),
        in_specs=[pl.BlockSpec((1, tp, d), lambda i: (i, 0, 0)), _resident((1, d))],
        out_specs=pl.BlockSpec((1, seq, d), lambda i: (i, 0, 0)),
        out_shape=jax.ShapeDtypeStruct((b, seq, d), F32),
        compiler_params=_cparams(1), name="final_norm")(h3, g.reshape(1, d))


def _rope_table_kernel(freq_ref, cos_ref, sin_ref):
    tp = cos_ref.shape[0]
    pos = (lax.broadcasted_iota(jnp.int32, (tp, LANES), 0) - PAD).astype(F32)
    lane = lax.broadcasted_iota(jnp.int32, (tp, LANES), 1) % DIFF_HEADDIM
    ang = pos * freq_ref[...]
    half = ROPE_DIM // 2
    rot = lane < ROPE_DIM
    cos_ref[...] = jnp.where(rot, jnp.cos(ang), 1.0)
    sin_ref[...] = jnp.where(lane < half, -jnp.sin(ang), jnp.where(rot, jnp.sin(ang), 0.0))


def rope_tables(tp):
    inv_freq = ROPE_THETA ** (-jnp.arange(0, ROPE_DIM, 2, dtype=F32) / ROPE_DIM)
    per_sub = jnp.concatenate([inv_freq, inv_freq, jnp.zeros((DIFF_HEADDIM - ROPE_DIM,), F32)])
    freq = jnp.concatenate([per_sub, per_sub]).reshape(1, LANES)
    return pl.pallas_call(
        _rope_table_kernel, grid=(1,),
        in_specs=[_resident((1, LANES))],
        out_specs=[pl.BlockSpec((tp, LANES), lambda i: (0, 0))] * 2,
        out_shape=[jax.ShapeDtypeStruct((tp, LANES), F32)] * 2,
        compiler_params=_cparams(1), name="rope_tables")(freq)


def _swap_matrix():
    src = lax.broadcasted_iota(jnp.int32, (LANES, LANES), 0)
    dst = lax.broadcasted_iota(jnp.int32, (LANES, LANES), 1)
    sub = dst % DIFF_HEADDIM
    half = ROPE_DIM // 2
    partner = jnp.where(sub < half, dst + half, jnp.where(sub < ROPE_DIM, dst - half, dst))
    return jnp.where(src == partner, 1.0, 0.0).astype(BF16)


def _rotate(x_bf16, cos, sin_signed, swap):
    return x_bf16.astype(F32) * cos + _dot(x_bf16, swap) * sin_signed


def _diff_attn_kernel(q_ref, k_ref, v_ref, cos_ref, sin_ref, lam_ref, w_ref, o_ref, kr_ref, s_ref,
                      *, lambda_init, tq):
    tp = k_ref.shape[1]
    swap = _swap_matrix()
    kr_ref[...] = _rotate(k_ref[0], cos_ref[...], sin_ref[...], swap).astype(BF16)
    scale = DIFF_HEADDIM ** -0.5 * math.log2(math.e)
    sub_head = lax.broadcasted_iota(jnp.int32, (tq, LANES), 1) // DIFF_HEADDIM
    real_key = lax.broadcasted_iota(jnp.int32, (tq, CHUNK), 1) >= PAD
    lp = lam_ref[...]
    lam = (jnp.exp(jnp.sum(lp[0:1] * lp[1:2], axis=-1, keepdims=True))
           - jnp.exp(jnp.sum(lp[2:3] * lp[3:4], axis=-1, keepdims=True)) + lambda_init)

    def stage_scores(qi, slot):
        rows = slice(qi * tq, (qi + 1) * tq)
        qr = (_rotate(q_ref[0, rows, :], cos_ref[rows, :], sin_ref[rows, :], swap) * scale).astype(BF16)
        for m in range(2):
            qm = jnp.where(sub_head == m, qr, jnp.zeros_like(qr))
            s_ref[slot, m] = _dot_nt(qm, kr_ref[...])

    def stage_finish(qi, slot):
        rows = slice(qi * tq, (qi + 1) * tq)
        maps = []
        for m in range(2):
            sa = jnp.where(real_key, s_ref[slot, m, :, 0:CHUNK], -jnp.inf)
            sb = s_ref[slot, m, :, CHUNK:]
            mx = jnp.maximum(jnp.max(sa, axis=-1, keepdims=True), jnp.max(sb, axis=-1, keepdims=True))
            pa = jnp.exp2(sa - mx)
            pb = jnp.exp2(sb - mx)
            den = jnp.sum(pa, axis=-1, keepdims=True) + jnp.sum(pb, axis=-1, keepdims=True)
            pv = _dot(pa.astype(BF16), v_ref[0, 0:CHUNK, :]) + _dot(pb.astype(BF16), v_ref[0, CHUNK:, :])
            maps.append(pv * (1.0 / den))
        o = maps[0] - lam * maps[1]
        o_ref[0, rows, :] = (_rms(o, w_ref[...]) * (1.0 - lambda_init)).astype(o_ref.dtype)

    n_blocks = tp // tq
    stage_scores(0, 0)
    for qi in range(n_blocks):
        if qi + 1 < n_blocks:
            stage_scores(qi + 1, (qi + 1) % 2)
        stage_finish(qi, qi % 2)


def diff_attention(qkv3, cos, sin, lam_params, subln_w, lambda_init, tq=272):
    b, tp, _ = qkv3.shape
    hw = 2 * DIFF_HEADDIM
    assert hw == LANES and tp % tq == 0 and tq % 16 == 0
    head = lambda k: pl.BlockSpec((1, tp, hw), lambda i, h: (i, 0, k * DIFF_HEADS + h))
    return pl.pallas_call(
        functools.partial(_diff_attn_kernel, lambda_init=lambda_init, tq=tq),
        grid=(b, DIFF_HEADS),
        in_specs=[head(0), head(1), head(2), _resident((tp, LANES)), _resident((tp, LANES)),
                  _resident(lam_params.shape), _resident((1, hw))],
        out_specs=head(0),
        out_shape=jax.ShapeDtypeStruct((b, tp, D_DIFF), BF16),
        scratch_shapes=[pltpu.VMEM((tp, hw), BF16), pltpu.VMEM((2, 2, tq, tp), F32)],
        compiler_params=_cparams(2), name="diff_attention")(
            qkv3, qkv3, qkv3, cos, sin, lam_params, subln_w.reshape(1, hw))


def _softplus(x):
    return jnp.maximum(x, 0.0) + jnp.log1p(jnp.exp(-jnp.abs(x)))


def _ssd_kernel(zx_ref, dt_ref, cw_ref, cb_ref, dtb_ref, alog_ref, aexp_ref, dsk_ref, nw_ref,
                o_ref, xact_ref, yacc_ref, dtv_ref, st_ref):
    tp = zx_ref.shape[1]
    n_chunks = tp // CHUNK
    gn = SSD_GROUPS * SSD_STATE

    valid = lax.broadcasted_iota(jnp.int32, (tp, LANES), 0) >= PAD
    dsum = dsk_ref[0:1, :] + dsk_ref[1:2, :]
    for c0 in range(0, SSD_CONV_DIM, LANES):
        x = jnp.where(valid, zx_ref[0, :, D_SSD + c0:D_SSD + c0 + LANES].astype(F32), 0.0)
        acc = cb_ref[:, c0:c0 + LANES] + x * cw_ref[SSD_CONV // 2:SSD_CONV // 2 + 1, c0:c0 + LANES]
        for k in range(SSD_CONV):
            off = k - SSD_CONV // 2
            if off != 0:
                acc = acc + pltpu.roll(x, (-off) % tp, axis=0) * cw_ref[k:k + 1, c0:c0 + LANES]
        act = _silu(acc)
        xact_ref[:, c0:c0 + LANES] = act.astype(BF16)
        if c0 < D_SSD:
            yacc_ref[:, c0:c0 + LANES] = act * dsum[:, c0:c0 + LANES]
    dtv_ref[...] = jnp.where(valid, _softplus(dt_ref[0] + dtb_ref[...]), 0.0)
    st_ref[...] = jnp.zeros_like(st_ref)

    ii = lax.broadcasted_iota(jnp.int32, (CHUNK, CHUNK), 0)
    jj = lax.broadcasted_iota(jnp.int32, (CHUNK, CHUNK), 1)
    lower = jj <= ii
    upper = jj >= ii
    tril = jnp.where(lower, 1.0, 0.0).astype(BF16)
    triu = jnp.where(upper, 1.0, 0.0).astype(BF16)
    a_row = -jnp.exp(alog_ref[...])
    src_lane = lax.broadcasted_iota(jnp.int32, (LANES, D_SSD), 0)
    dst_head = lax.broadcasted_iota(jnp.int32, (LANES, D_SSD), 1) // SSD_HEADDIM
    half = lax.broadcasted_iota(jnp.int32, (CHUNK, LANES), 1) // SSD_HEADDIM

    def chunk_step(c, d):
        rows = pl.ds(pl.multiple_of(c * CHUNK, CHUNK), CHUNK)
        tri_col, tri_row, mask = (tril, triu, lower) if d == 0 else (triu, tril, upper)
        expand = jnp.where(src_lane == dst_head + SSD_HEADS * d, 1.0, 0.0).astype(BF16)
        dtc = dtv_ref[rows, :]
        a = dtc * a_row
        dt_exp = _dot3_right(dtc, expand)
        a_exp = dt_exp * (-jnp.exp(aexp_ref[d:d + 1, :]))
        col = _dot3_left(tri_col, a_exp)
        rowv = _dot3_right(a.T, tri_row)
        tot = col[CHUNK - 1:CHUNK, :] if d == 0 else col[0:1, :]
        ecol = jnp.exp(col)
        wst = jnp.exp(tot - col)
        dec = jnp.exp(tot)
        xdt = xact_ref[rows, 0:D_SSD].astype(F32) * dt_exp
        xdt_b = xdt.astype(BF16)
        xw = (xdt * wst).astype(BF16)
        for g in range(SSD_GROUPS):
            bg = xact_ref[rows, D_SSD + g * SSD_STATE:D_SSD + (g + 1) * SSD_STATE]
            cg = xact_ref[rows, D_SSD + gn + g * SSD_STATE:D_SSD + gn + (g + 1) * SSD_STATE]
            cb = _dot_nt(cg, bg)
            bt = bg.astype(F32).T.astype(BF16)
            for pp in range(2):
                p = 2 * g + pp
                lanes = slice(p * LANES, (p + 1) * LANES)
                s_prev = st_ref[d, p]
                y = _dot(cg, s_prev.astype(BF16)) * ecol[:, lanes]
                for hh in range(2):
                    h = 2 * p + hh
                    ccol = col[:, h * SSD_HEADDIM:h * SSD_HEADDIM + 1]
                    crow = rowv[h + SSD_HEADS * d:h + SSD_HEADS * d + 1, :]
                    lmat = jnp.exp(jnp.where(mask, ccol - crow, -jnp.inf))
                    rhs = jnp.where(half == hh, xdt_b[:, lanes], jnp.zeros((CHUNK, LANES), BF16))
                    y = y + _dot((cb * lmat).astype(BF16), rhs)
                yacc_ref[rows, lanes] += y
                st_ref[d, p] = s_prev * dec[:, lanes] + _dot(bt, xw[:, lanes])

    def scan_body(i, carry):
        chunk_step(i, 0)
        chunk_step(n_chunks - 1 - i, 1)
        return carry

    lax.fori_loop(0, n_chunks, scan_body, 0, unroll=2)

    gw = D_SSD // SSD_GROUPS

    def gate_body(c, carry):
        rows = pl.ds(pl.multiple_of(c * CHUNK, CHUNK), CHUNK)
        yz = yacc_ref[rows, :] * _silu(zx_ref[0, rows, 0:D_SSD].astype(F32))
        for g in range(SSD_GROUPS):
            seg = _rms(yz[:, g * gw:(g + 1) * gw], nw_ref[:, g * gw:(g + 1) * gw])
            o_ref[0, rows, g * gw:(g + 1) * gw] = seg.astype(o_ref.dtype)
        return carry

    lax.fori_loop(0, n_chunks, gate_body, 0)


def ssd_mixer(zx3, dt3, conv_w, conv_b, a_log, dt_bias, d_skip, norm_w):
    b, tp, wzx = zx3.shape
    lane_pad = LANES - 2 * SSD_HEADS
    dtb = jnp.pad(dt_bias.reshape(1, -1), ((0, 0), (0, lane_pad)))
    alog = jnp.pad(a_log.reshape(1, -1), ((0, 0), (0, lane_pad)))
    aexp = jnp.repeat(a_log, SSD_HEADDIM, axis=1)
    dsk = jnp.repeat(d_skip, SSD_HEADDIM, axis=1)
    return pl.pallas_call(
        _ssd_kernel, grid=(b,),
        in_specs=[pl.BlockSpec((1, tp, wzx), lambda i: (i, 0, 0)),
                  pl.BlockSpec((1, tp, LANES), lambda i: (i, 0, 0)),
                  _resident(conv_w.shape), _resident((1, SSD_CONV_DIM)),
                  _resident((1, LANES)), _resident((1, LANES)),
                  _resident((2, D_SSD)), _resident((2, D_SSD)), _resident((1, D_SSD))],
        out_specs=pl.BlockSpec((1, tp, D_SSD), lambda i: (i, 0, 0)),
        out_shape=jax.ShapeDtypeStruct((b, tp, D_SSD), BF16),
        scratch_shapes=[pltpu.VMEM((tp, SSD_CONV_DIM), BF16), pltpu.VMEM((tp, D_SSD), F32),
                        pltpu.VMEM((tp, LANES), F32),
                        pltpu.VMEM((2, SSD_HEADS // 2, SSD_STATE, LANES), F32)],
        compiler_params=_cparams(1), name="ssd_mixer")(
            zx3, dt3, conv_w, conv_b.reshape(1, -1), dtb, alog, aexp, dsk, norm_w.reshape(1, -1))


def _hy_conv_kernel(p0_ref, p1_ref, p2_ref, w0_ref, w1_ref, w2_ref, b0_ref, b1_ref, b2_ref,
                    x0_ref, v_ref):
    tp = p0_ref.shape[1]
    ct = p0_ref.shape[2]
    valid = lax.broadcasted_iota(jnp.int32, (tp, LANES), 0) >= PAD

    def conv(p_ref, w_ref, b_ref, c0):
        x = jnp.where(valid, p_ref[0, :, c0:c0 + LANES].astype(F32), 0.0)
        acc = b_ref[:, c0:c0 + LANES] + x * w_ref[HYENA_SHORT // 2:HYENA_SHORT // 2 + 1, c0:c0 + LANES]
        for k in range(HYENA_SHORT):
            off = k - HYENA_SHORT // 2
            if off != 0:
                acc = acc + pltpu.roll(x, (-off) % tp, axis=0) * w_ref[k:k + 1, c0:c0 + LANES]
        return acc

    for c0 in range(0, ct, LANES):
        x0_ref[0, :, c0:c0 + LANES] = conv(p0_ref, w0_ref, b0_ref, c0).astype(x0_ref.dtype)
        v = conv(p2_ref, w2_ref, b2_ref, c0) * conv(p1_ref, w1_ref, b1_ref, c0)
        v_ref[0, :, c0:c0 + LANES] = jnp.where(valid, v, 0.0).astype(v_ref.dtype)


def hyena_conv_gate(p3, conv_w, conv_b, ct=256):
    b, tp, d3 = p3.shape
    d = d3 // 3
    nct = d // ct
    conv_b = conv_b.reshape(1, d3)
    pspec = lambda k: pl.BlockSpec((1, tp, ct), lambda i, j: (i, 0, j + k * nct))
    wspec = lambda k: pl.BlockSpec((HYENA_SHORT, ct), lambda i, j: (0, j + k * nct))
    bspec = lambda k: pl.BlockSpec((1, ct), lambda i, j: (0, j + k * nct))
    ospec = pl.BlockSpec((1, tp, ct), lambda i, j: (i, 0, j))
    return pl.pallas_call(
        _hy_conv_kernel, grid=(b, nct),
        in_specs=[pspec(0), pspec(1), pspec(2), wspec(0), wspec(1), wspec(2),
                  bspec(0), bspec(1), bspec(2)],
        out_specs=[ospec, ospec],
        out_shape=[jax.ShapeDtypeStruct((b, tp, d), BF16)] * 2,
        compiler_params=_cparams(2), name="hyena_conv_gate")(
            p3, p3, p3, conv_w, conv_w, conv_w, conv_b, conv_b, conv_b)


def _dot_f32(a, b):
    return jnp.dot(a, b, precision=lax.Precision.HIGHEST, preferred_element_type=F32)


def _hy_filter_kernel(w1t_ref, w1c_ref, w1s_ref, b1_ref, w2_ref, b2_ref, w3_ref, b3_ref, fr_ref,
                      w4f_ref, w4b_ref, bands_ref, deltas_ref, hf_ref, hb_ref, *, t_len):
    tp = hf_ref.shape[0]
    row = lax.broadcasted_iota(jnp.int32, (tp, 1), 0)
    pos = row.astype(F32)
    t = pos / (t_len - 1)
    ang = (2.0 * math.pi * pos / t_len) * bands_ref[...]
    h = t * w1t_ref[...] + _dot_f32(jnp.cos(ang), w1c_ref[...]) + _dot_f32(-jnp.sin(ang), w1s_ref[...])
    h = jnp.sin(fr_ref[0:1, :] * (h + b1_ref[...]))
    h = jnp.sin(fr_ref[1:2, :] * (_dot_f32(h, w2_ref[...]) + b2_ref[...]))
    h = jnp.sin(fr_ref[2:3, :] * (_dot_f32(h, w3_ref[...]) + b3_ref[...]))
    decay = jnp.exp(-t * deltas_ref[...]) + HYENA_SHIFT
    hf = jnp.where(row < t_len, _dot_f32(h, w4f_ref[...]) * decay, 0.0)
    hb = jnp.where((row >= 1) & (row < t_len), _dot_f32(h, w4b_ref[...]) * decay, 0.0)
    norm = (jnp.sum(jnp.abs(hf), axis=0, keepdims=True)
            + jnp.sum(jnp.abs(hb), axis=0, keepdims=True))
    hf_ref[...] = hf / norm
    hb_ref[...] = hb / norm


def hyena_filters(tp, t_len, w1, b1, w2, b2, w3, b3, w4, freq, ct=256):
    d = w4.shape[1] // 2
    fw = w1.shape[1]
    nct = d // ct
    bands = jnp.linspace(1e-4, HYENA_BANDS - 1, HYENA_BANDS, dtype=F32).reshape(1, -1)
    max_decay = math.log(HYENA_TARGET) / HYENA_FAST_DECAY
    min_decay = math.log(HYENA_TARGET) / HYENA_SLOW_DECAY
    deltas = jnp.abs(jnp.linspace(min_decay, max_decay, d, dtype=F32)).reshape(1, -1)
    nb = HYENA_BANDS
    full = lambda a: pl.BlockSpec(a.shape, lambda j: (0,) * a.ndim)
    args = [w1[0:1], w1[1:1 + nb], w1[1 + nb:], b1.reshape(1, fw), w2, b2.reshape(1, fw),
            w3, b3.reshape(1, fw), freq]
    return pl.pallas_call(
        functools.partial(_hy_filter_kernel, t_len=t_len), grid=(nct,),
        in_specs=[full(a) for a in args] + [
            pl.BlockSpec((fw, ct), lambda j: (0, j)), pl.BlockSpec((fw, ct), lambda j: (0, j + nct)),
            full(bands), pl.BlockSpec((1, ct), lambda j: (0, j))],
        out_specs=[pl.BlockSpec((tp, ct), lambda j: (0, j))] * 2,
        out_shape=[jax.ShapeDtypeStruct((tp, d), F32)] * 2,
        compiler_params=_cparams(1), name="hyena_filters")(*args, w4, w4, bands, deltas)


def _dft_table_kernel(cs_ref, *, period):
    tr = cs_ref.shape[0]
    tp = cs_ref.shape[1] // 2
    a = lax.broadcasted_iota(jnp.int32, (tr, tp), 0) + pl.program_id(0) * tr
    b = lax.broadcasted_iota(jnp.int32, (tr, tp), 1)
    n = (2 * a + 1) * (2 * b + 1)
    q = jnp.floor(n.astype(F32) * (1.0 / period)).astype(jnp.int32)
    r = (n - q * period).astype(F32)
    ang = r * (2.0 * math.pi / period)
    cs_ref[:, :tp] = jnp.cos(ang).astype(cs_ref.dtype)
    cs_ref[:, tp:] = jnp.sin(ang).astype(cs_ref.dtype)


def dft_tables(tp, tr=272):
    return pl.pallas_call(
        functools.partial(_dft_table_kernel, period=8 * tp), grid=(tp // tr,),
        out_specs=pl.BlockSpec((tr, 2 * tp), lambda i: (i, 0)),
        out_shape=jax.ShapeDtypeStruct((tp, 2 * tp), BF16),
        compiler_params=_cparams(1), name="dft_tables")()


def _split2(x):
    hi = x.astype(BF16)
    return hi, (x - hi.astype(F32)).astype(BF16)


def _hy_spectrum_kernel(hf_ref, hb_ref, cs_ref, p_ref, q_ref, *, mc):
    tp = hf_ref.shape[0]
    big_l = 2 * tp
    scale = 2.0 / big_l
    hs_hi, hs_lo = _split2(hf_ref[...] + hb_ref[...])
    hd_hi, hd_lo = _split2(hf_ref[...] - hb_ref[...])
    for m0 in range(0, tp, mc):
        rows = slice(m0, m0 + mc)
        phi = ((lax.broadcasted_iota(jnp.int32, (mc, 1), 0) + m0).astype(F32) + 0.5) * (math.pi / big_l)
        cphi, sphi = jnp.cos(phi), jnp.sin(phi)
        cm, sm = cs_ref[rows, :tp], cs_ref[rows, tp:]
        c_hs = _dot(cm, hs_hi) + _dot(cm, hs_lo)
        s_hs = _dot(sm, hs_hi) + _dot(sm, hs_lo)
        c_hd = _dot(cm, hd_hi) + _dot(cm, hd_lo)
        s_hd = _dot(sm, hd_hi) + _dot(sm, hd_lo)
        p_ref[rows, :] = (c_hs * cphi + s_hs * sphi) * scale
        q_ref[rows, :] = (s_hd * cphi - c_hd * sphi) * scale


def hyena_spectrum(hf, hb, cs, ct=256, mc=272):
    tp, d = hf.shape
    cspec = pl.BlockSpec((tp, ct), lambda j: (0, j))
    return pl.pallas_call(
        functools.partial(_hy_spectrum_kernel, mc=mc), grid=(d // ct,),
        in_specs=[cspec, cspec, _resident(cs.shape)],
        out_specs=[cspec, cspec],
        out_shape=[jax.ShapeDtypeStruct((tp, d), F32)] * 2,
        compiler_params=_cparams(1), name="hyena_spectrum")(hf, hb, cs)


def _hy_longconv_kernel(v_ref, x0_ref, p_ref, q_ref, skip_ref, cs_ref, o_ref, ycs_ref, *, mc):
    tp = v_ref.shape[1]
    v = v_ref[0]
    for m0 in range(0, tp, mc):
        rows = slice(m0, m0 + mc)
        a = _dot(cs_ref[rows, :tp], v)
        b = _dot(cs_ref[rows, tp:], v)
        p, q = p_ref[rows, :], q_ref[rows, :]
        ycs_ref[m0:m0 + mc, :] = (a * p - b * q).astype(BF16)
        ycs_ref[tp + m0:tp + m0 + mc, :] = (a * q + b * p).astype(BF16)
    for m0 in range(0, tp, mc):
        rows = slice(m0, m0 + mc)
        y = _dot(cs_ref[rows, :], ycs_ref[...])
        y = (y + v_ref[0, rows, :].astype(F32) * skip_ref[...]) * x0_ref[0, rows, :].astype(F32)
        o_ref[0, rows, :] = y.astype(o_ref.dtype)


def hyena_longconv(v3, x03, p, q, skip, cs, ct=256, mc=272):
    b, tp, d = v3.shape
    bspec = pl.BlockSpec((1, tp, ct), lambda j, i: (i, 0, j))
    cspec = pl.BlockSpec((tp, ct), lambda j, i: (0, j))
    return pl.pallas_call(
        functools.partial(_hy_longconv_kernel, mc=mc), grid=(d // ct, b),
        in_specs=[bspec, bspec, cspec, cspec, pl.BlockSpec((1, ct), lambda j, i: (0, j)),
                  _resident(cs.shape)],
        out_specs=bspec,
        out_shape=jax.ShapeDtypeStruct((b, tp, d), BF16),
        scratch_shapes=[pltpu.VMEM((2 * tp, ct), BF16)],
        compiler_params=_cparams(2), name="hyena_longconv")(
            v3, x03, p, q, skip.reshape(1, d), cs)


HI_MASK = 0xFFFF0000


def _pack_halves(x):
    w = x.shape[1] // 2
    lo = pltpu.bitcast(x[:, :w].astype(BF16).astype(F32), jnp.uint32)
    hi = pltpu.bitcast(x[:, w:].astype(BF16).astype(F32), jnp.uint32)
    return (lo >> 16) | (hi & jnp.uint32(HI_MASK))


def _unpack_halves(w):
    return (pltpu.bitcast(w << 16, F32), pltpu.bitcast(w & jnp.uint32(HI_MASK), F32))


def _router_kernel(x_ref, a_ref, wo_ref, bo_ref, g_ref, wr_ref, valid_ref,
                   h_ref, xp_ref, gate_ref, kind_ref, cnt_ref, carry_ref):
    @pl.when(pl.program_id(0) == 0)
    def _():
        carry_ref[...] = jnp.zeros_like(carry_ref)

    h = x_ref[...] + _dot(a_ref[...], wo_ref[...]) + bo_ref[...]
    h_ref[...] = h
    xn = _rms(h, g_ref[...])
    xp_ref[...] = _pack_halves(xn)
    tm = xn.shape[0]
    lane = lax.broadcasted_iota(jnp.int32, (tm, LANES), 1)
    x_hi, x_lo = _split2(xn)
    w_hi, w_lo = _split2(wr_ref[...])
    logits = _dot(x_hi, w_hi) + _dot(x_lo, w_hi) + _dot(x_hi, w_lo)
    logits = jnp.where(lane < N_EXPERTS, logits, -jnp.inf)
    v1 = jnp.max(logits, axis=-1, keepdims=True)
    i1 = jnp.min(jnp.where(logits == v1, lane, LANES), axis=-1, keepdims=True)
    rest = jnp.where(lane == i1, -jnp.inf, logits)
    v2 = jnp.max(rest, axis=-1, keepdims=True)
    i2 = jnp.min(jnp.where(rest == v2, lane, LANES), axis=-1, keepdims=True)
    e2 = jnp.exp(v2 - v1)
    den = 1.0 + e2
    gates = jnp.where(lane == i1, 1.0 / den, 0.0) + jnp.where(lane == i2, e2 / den, 0.0)
    kind = jnp.where(lane == i1, 1.0, 0.0) + jnp.where(lane == i2, 2.0, 0.0)
    valid = valid_ref[...]
    gate_t = gates.T[0:N_EXPERTS, :] * valid
    kind_t = kind.T[0:N_EXPERTS, :] * valid
    gate_ref[...] = gate_t
    kind_ref[...] = kind_t
    member = jnp.where(kind_t > 0.0, 1.0, 0.0)
    earlier = (lax.broadcasted_iota(jnp.int32, (tm, tm), 0)
               <= lax.broadcasted_iota(jnp.int32, (tm, tm), 1))
    running = _dot(member.astype(BF16), jnp.where(earlier, 1.0, 0.0).astype(BF16))
    cnt_ref[...] = running + carry_ref[:, 0:1]
    carry_ref[...] = carry_ref[...] + jnp.sum(member, axis=-1, keepdims=True)


def route_top2(x, a, wo, bo, g, router, tp, tm=512):
    n_rows, d = x.shape
    tm = _row_tile(n_rows, tm)
    wr = jnp.pad(router, ((0, 0), (0, LANES - router.shape[1])))
    valid = ((jnp.arange(n_rows, dtype=jnp.int32) % tp) >= PAD).astype(F32).reshape(1, n_rows)
    row = lambda w: pl.BlockSpec((tm, w), lambda i: (i, 0))
    espec = pl.BlockSpec((N_EXPERTS, tm), lambda i: (0, i))
    eshape = jax.ShapeDtypeStruct((N_EXPERTS, n_rows), F32)
    return pl.pallas_call(
        _router_kernel, grid=(n_rows // tm,),
        in_specs=[row(d), row(a.shape[1]), _resident(wo.shape), _resident((1, d)),
                  _resident((1, d)), _resident(wr.shape), pl.BlockSpec((1, tm), lambda i: (0, i))],
        out_specs=[row(d), row(d // 2), espec, espec, espec],
        out_shape=[jax.ShapeDtypeStruct((n_rows, d), F32),
                   jax.ShapeDtypeStruct((n_rows, d // 2), jnp.uint32), eshape, eshape, eshape],
        scratch_shapes=[pltpu.VMEM((N_EXPERTS, LANES), F32)],
        compiler_params=_cparams(1), name="moe_router")(
            x, a, wo, bo.reshape(1, d), g.reshape(1, d), wr, valid)


def _route_invert_kernel(te_ref, q0_ref, lo_ref, hi_ref, cnt_ref, gate_ref, kind_ref, trash_ref,
                         tok_ref, slot_ref, gcol_ref, cols_ref, *, n_rows, n_win):
    j = pl.program_id(0)
    e = te_ref[j]
    tm = tok_ref.shape[2]
    out_lane = lax.broadcasted_iota(jnp.int32, (SUB_ROWS, LANES), 1)
    total = cnt_ref[e, :, n_rows - 1:n_rows]
    for s in range(tm // SUB_ROWS):
        first = q0_ref[j] + s * SUB_ROWS + 1
        want = (first + lax.broadcasted_iota(jnp.int32, (SUB_ROWS, 1), 0)).astype(F32)
        lo = lo_ref[j * (tm // SUB_ROWS) + s]

        def block(k, acc, want=want):
            below, weight, second = acc
            cols = pl.ds(pl.multiple_of(k * LANES, LANES), LANES)
            c = cnt_ref[e, :, cols]
            kd = kind_ref[e, :, cols]
            below = below + jnp.where(c < want, 1.0, 0.0)
            hit = jnp.where(kd > 0.0, c, -1.0) == want
            weight = weight + jnp.where(hit, gate_ref[e, :, cols], 0.0)
            second = second + jnp.where(hit, jnp.where(kd == 2.0, 1.0, 0.0), 0.0)
            return below, weight, second

        zero = jnp.zeros((SUB_ROWS, LANES), F32)
        below, weight, second = lax.fori_loop(lo, hi_ref[j * (tm // SUB_ROWS) + s], block,
                                              (zero, zero, zero))
        rows = slice(s * SUB_ROWS, (s + 1) * SUB_ROWS)
        gcol_ref[rows, :] = jnp.sum(weight, axis=-1, keepdims=True)
        tok_col = (lo * LANES).astype(F32) + jnp.sum(below, axis=-1, keepdims=True)
        cols_ref[rows, :] = jnp.where(out_lane == 0, tok_col, jnp.where(
            out_lane == 1, jnp.sum(second, axis=-1, keepdims=True), jnp.where(
                out_lane == 2, jnp.where(want <= total, 1.0, 0.0), 0.0)))
    t = cols_ref[...].T
    found = t[2:3, :] > 0.5
    tok = jnp.where(found, t[0:1, :], 0.0).astype(jnp.int32)
    plane = jnp.where(t[1:2, :] > 0.5, n_rows, 0)
    trash = trash_ref[:, pl.ds(pl.multiple_of((j % n_win) * tm, tm), tm)]
    tok_ref[0] = tok
    slot_ref[0] = jnp.where(found, plane + tok, trash)


def route_plan(gate_t, kind_t, cnt_t, b, tp, tm):
    n = b * tp
    n_tiles = (TOP_K * b * (tp - PAD)) // tm + N_EXPERTS
    n_blk = n // LANES
    counts = cnt_t[:, -1].astype(jnp.int32)
    tiles = (counts + tm - 1) // tm
    tile_end = jnp.cumsum(tiles)
    tile_id = jnp.arange(n_tiles, dtype=jnp.int32)
    te = jnp.minimum(jnp.sum(tile_id[:, None] >= tile_end[None, :], axis=1), N_EXPERTS - 1)
    te = te.astype(jnp.int32)
    q0 = (tile_id - (tile_end - tiles)[te]) * tm
    n_sub = tm // SUB_ROWS
    block_end = cnt_t[:, LANES - 1::LANES].astype(jnp.int32)[te]
    first = q0[:, None] + jnp.arange(n_sub, dtype=jnp.int32)[None, :] * SUB_ROWS + 1
    last = jnp.minimum(first + SUB_ROWS - 1, counts[te][:, None])
    lo = jnp.sum(block_end[:, None, :] < first[:, :, None], axis=2)
    hi = jnp.minimum(jnp.sum(block_end[:, None, :] < last[:, :, None], axis=2) + 1, n_blk)
    hi = jnp.maximum(hi, lo).reshape(-1)
    lo = lo.reshape(-1)
    n_pad_rows = TOP_K * b * PAD
    n_win = max(1, n_pad_rows // tm)
    p = jnp.arange(n_win * tm, dtype=jnp.int32) % n_pad_rows
    trash = ((p // (b * PAD)) * n + ((p % (b * PAD)) // PAD) * tp + p % PAD).reshape(1, n_win * tm)
    grid_spec = pltpu.PrefetchScalarGridSpec(
        num_scalar_prefetch=4, grid=(n_tiles,),
        in_specs=[_resident((N_EXPERTS, 1, n))] * 3 + [_resident(trash.shape)],
        out_specs=[pl.BlockSpec((1, 1, tm), lambda j, *_: (j, 0, 0)),
                   pl.BlockSpec((1, 1, tm), lambda j, *_: (j, 0, 0)),
                   pl.BlockSpec((tm, 1), lambda j, *_: (j, 0))],
        scratch_shapes=[pltpu.VMEM((tm, LANES), F32)])
    tok, slot, gate = pl.pallas_call(
        functools.partial(_route_invert_kernel, n_rows=n, n_win=n_win), grid_spec=grid_spec,
        out_shape=[jax.ShapeDtypeStruct((n_tiles, 1, tm), jnp.int32),
                   jax.ShapeDtypeStruct((n_tiles, 1, tm), jnp.int32),
                   jax.ShapeDtypeStruct((n_tiles * tm, 1), F32)],
        compiler_params=_cparams(1), name="route_invert")(
            te, q0.astype(jnp.int32), lo.astype(jnp.int32), hi.astype(jnp.int32),
            cnt_t.reshape(N_EXPERTS, 1, n), gate_t.reshape(N_EXPERTS, 1, n),
            kind_t.reshape(N_EXPERTS, 1, n), trash)
    trash_tile = trash[:, (n_win - 1) * tm:].reshape(1, 1, tm)
    return tok, slot, gate, te, trash_tile


def _moe_sparse_kernel(te_ref, tokc_ref, tok1_ref, slotp_ref, slotc_ref, gate_ref, xp_hbm,
                       wg_ref, wu_ref, wd_ref, out_hbm, gbuf, ybuf, hm_ref, zbuf, gsem, ssem, zsem,
                       *, tf, tp):
    j = pl.program_id(0)
    tm = gbuf.shape[1]
    cur = j % 2

    @pl.when(j == 0)
    def _():
        zbuf[...] = jnp.zeros_like(zbuf)
        n_seq = out_hbm.shape[0] // tp
        copies = [pltpu.make_async_copy(zbuf, out_hbm.at[pl.ds(s * tp, PAD)], zsem)
                  for s in range(n_seq)]
        for c in copies:
            c.start()
        for c in copies:
            c.wait()

    prev = 1 - cur
    last = pl.num_programs(0) - 1

    def gather_row(tok_ref, buf_slot, r):
        return pltpu.make_async_copy(xp_hbm.at[pl.ds(tok_ref[0, 0, r], 1)],
                                     gbuf.at[buf_slot, pl.ds(r, 1)], gsem.at[buf_slot])

    def scatter_row(dst_ref, buf_slot, r):
        return pltpu.make_async_copy(ybuf.at[buf_slot, pl.ds(r, 1)],
                                     out_hbm.at[pl.ds(dst_ref[0, 0, r], 1)], ssem.at[buf_slot])

    def wait_gather(buf_slot):
        pltpu.make_async_copy(xp_hbm.at[pl.ds(0, tm)], gbuf.at[buf_slot], gsem.at[buf_slot]).wait()

    def wait_scatter(buf_slot):
        pltpu.make_async_copy(ybuf.at[buf_slot], out_hbm.at[pl.ds(0, tm)], ssem.at[buf_slot]).wait()

    @pl.when(j == 0)
    def _():
        ybuf[1] = jnp.zeros((tm, ybuf.shape[2]), ybuf.dtype)

        def body(r, carry):
            gather_row(tokc_ref, 0, r).start()
            return carry
        lax.fori_loop(0, tm, body, 0, unroll=8)

    wait_gather(cur)
    lo, hi = _unpack_halves(gbuf[cur])
    x = jnp.concatenate([lo.astype(BF16), hi.astype(BF16)], axis=1)
    gate = gate_ref[...]
    f = wg_ref.shape[2]
    n_chunks = f // tf
    moves = ([functools.partial(gather_row, tok1_ref, prev, r) for r in range(tm)]
             + [functools.partial(scatter_row, slotp_ref, prev, r) for r in range(tm)])
    per_chunk = -(-len(moves) // n_chunks)
    for ci in range(n_chunks):
        c0 = ci * tf
        gt = _dot(x, wg_ref[0, :, c0:c0 + tf])
        ut = _dot(x, wu_ref[0, :, c0:c0 + tf])
        hm_ref[:, c0:c0 + tf] = (_silu(gt) * ut * gate).astype(BF16)
        for move in moves[ci * per_chunk:(ci + 1) * per_chunk]:
            move().start()
    y = _pack_halves(_dot(hm_ref[...], wd_ref[0]))

    @pl.when(j >= 1)
    def _():
        wait_scatter(cur)

    ybuf[cur] = y

    @pl.when(j == last)
    def _():
        def body(r, carry):
            scatter_row(slotc_ref, cur, r).start()
            return carry
        lax.fori_loop(0, tm, body, 0, unroll=8)
        wait_scatter(prev)
        wait_scatter(cur)
        wait_gather(prev)


def moe_sparse(xp, plan, wg, wu, wd, tp, tf=512):
    tok, slot, gate, tile_expert, trash_tile = plan
    n_tiles, _, tm = tok.shape
    n_rows, dh = xp.shape
    n_e, d, fe = wg.shape
    slot = jnp.concatenate([trash_tile, slot])
    smem_tile = lambda imap: pl.BlockSpec((1, 1, tm), imap, memory_space=pltpu.SMEM)
    wspec = lambda shape: pl.BlockSpec(shape, lambda j, te: (te[j], 0, 0),
                                       pipeline_mode=pl.Buffered(1))
    grid_spec = pltpu.PrefetchScalarGridSpec(
        num_scalar_prefetch=1, grid=(n_tiles,),
        in_specs=[smem_tile(lambda j, te: (j, 0, 0)),
                  smem_tile(lambda j, te: (jnp.minimum(j + 1, n_tiles - 1), 0, 0)),
                  smem_tile(lambda j, te: (j, 0, 0)),
                  smem_tile(lambda j, te: (j + 1, 0, 0)),
                  pl.BlockSpec((tm, 1), lambda j, te: (j, 0)),
                  pl.BlockSpec(memory_space=pl.ANY),
                  wspec((1, d, fe)), wspec((1, d, fe)), wspec((1, fe, d))],
        out_specs=pl.BlockSpec(memory_space=pl.ANY),
        scratch_shapes=[pltpu.VMEM((2, tm, dh), jnp.uint32), pltpu.VMEM((2, tm, dh), jnp.uint32),
                        pltpu.VMEM((tm, fe), BF16), pltpu.VMEM((PAD, dh), jnp.uint32),
                        pltpu.SemaphoreType.DMA((2,)), pltpu.SemaphoreType.DMA((2,)),
                        pltpu.SemaphoreType.DMA(())])
    return pl.pallas_call(
        functools.partial(_moe_sparse_kernel, tf=tf, tp=tp), grid_spec=grid_spec,
        out_shape=jax.ShapeDtypeStruct((TOP_K * n_rows, dh), jnp.uint32),
        compiler_params=_cparams(1), name="moe_sparse")(
            tile_expert, tok, tok, slot, slot, gate, xp, wg, wu, wd)


def _moe_combine_norm_kernel(h_ref, y0_ref, y1_ref, g_ref, o_ref):
    dh = y0_ref.shape[3]
    a_lo, a_hi = _unpack_halves(y0_ref[0, 0])
    b_lo, b_hi = _unpack_halves(y1_ref[0, 0])
    lo = h_ref[0, :, :dh] + a_lo + b_lo
    hi = h_ref[0, :, dh:] + a_hi + b_hi
    ms = (jnp.sum(lo * lo, axis=-1, keepdims=True)
          + jnp.sum(hi * hi, axis=-1, keepdims=True)) / (2 * dh)
    inv = lax.rsqrt(ms + EPS)
    o_ref[0, :, :dh] = lo * inv * g_ref[:, :dh]
    o_ref[0, :, dh:] = hi * inv * g_ref[:, dh:]


def moe_combine_final_norm(h3, y2, g, seq):
    b, tp, d = h3.shape
    assert seq % CHUNK == 0 and tp == seq + CHUNK
    y4 = y2.reshape(TOP_K, b, tp, d // 2)
    yspec = lambda k: pl.BlockSpec((1, 1, CHUNK, d // 2), lambda i, j: (k, i, j + 1, 0))
    return pl.pallas_call(
        _moe_combine_norm_kernel, grid=(b, seq // CHUNK),
        in_specs=[pl.BlockSpec((1, CHUNK, d), lambda i, j: (i, j + 1, 0)), yspec(0), yspec(1),
                  _resident((1, d))],
        out_specs=pl.BlockSpec((1, CHUNK, d), lambda i, j: (i, j, 0)),
        out_shape=jax.ShapeDtypeStruct((b, seq, d), F32),
        compiler_params=_cparams(2), name="moe_combine_norm")(h3, y4, y4, g.reshape(1, d))


def kernel(x, meta_tokens, norm_mix_even, w_in_ab, ssd_conv_w, ssd_conv_b, ssd_a_log,
           ssd_dt_bias, ssd_d, ssd_norm_w, diff_lambda, diff_subln_w, w_out_ab,
           norm_ffn_even, ffn_w_gate, ffn_w_up, ffn_w_down, norm_mix_odd, hy_w_in, hy_b_in,
           hy_conv_w, hy_conv_b, hy_f_w1, hy_f_b1, hy_f_w2, hy_f_b2, hy_f_w3, hy_f_b3,
           hy_f_w4, hy_f_freq, hy_skip, hy_w_out, hy_b_out, norm_ffn_odd, moe_router,
           moe_w_gate, moe_w_up, moe_w_down, final_norm):
    b, seq, d = x.shape
    assert d == D_MODEL and seq % 512 == 0
    tp = seq + CHUNK
    t_len = seq + N_META
    n = b * tp
    meta = jnp.broadcast_to(meta_tokens[None].astype(x.dtype), (b, N_META, d))
    h = jnp.concatenate([jnp.zeros((b, PAD, d), x.dtype), meta, x], axis=1).reshape(n, d)
    cos, sin = rope_tables(tp)
    depth = norm_mix_even.shape[0] + norm_mix_odd.shape[0]
    o1 = D_SSD + SSD_CONV_DIM
    o2 = o1 + 2 * SSD_HEADS
    for layer in range(depth):
        i = layer // 2
        if layer % 2 == 0:
            w = w_in_ab[i]
            w_dt = jnp.pad(w[:, o1:o2], ((0, 0), (0, LANES - 2 * SSD_HEADS)))
            zx, dt, qkv = rms_proj(
                h, norm_mix_even[i],
                [w[:, :o1].astype(BF16), w_dt.astype(BF16), w[:, o2:].astype(BF16)],
                None, [BF16, F32, BF16])
            ssd_out = ssd_mixer(zx.reshape(b, tp, -1), dt.reshape(b, tp, -1), ssd_conv_w[i],
                                ssd_conv_b[i], ssd_a_log[i], ssd_dt_bias[i], ssd_d[i],
                                ssd_norm_w[i])
            lambda_init = 0.8 - 0.6 * math.exp(-0.3 * layer)
            attn_out = diff_attention(qkv.reshape(b, tp, -1), cos, sin, diff_lambda[i],
                                      diff_subln_w[i], lambda_init)
            wo = w_out_ab[i].astype(BF16)
            h = ffn_residual(h, [ssd_out.reshape(n, -1), attn_out.reshape(n, -1)],
                             [wo[:D_SSD], wo[D_SSD:]], norm_ffn_even[i],
                             ffn_w_gate[i].astype(BF16), ffn_w_up[i].astype(BF16),
                             ffn_w_down[i].astype(BF16))
        else:
            (p,) = rms_proj(h, norm_mix_odd[i], [hy_w_in[i].astype(BF16)],
                            [hy_b_in[i].reshape(1, -1)], [BF16])
            x0, v = hyena_conv_gate(p.reshape(b, tp, -1), hy_conv_w[i], hy_conv_b[i])
            hf, hb = hyena_filters(tp, t_len, hy_f_w1[i], hy_f_b1[i], hy_f_w2[i], hy_f_b2[i],
                                   hy_f_w3[i], hy_f_b3[i], hy_f_w4[i], hy_f_freq[i])
            cs = dft_tables(tp)
            fp, fq = hyena_spectrum(hf, hb, cs)
            y = hyena_longconv(v, x0, fp, fq, hy_skip[i], cs)
            assert layer == depth - 1, "an expert layer must close the trunk"
            h, xp, gate_t, kind_t, cnt_t = route_top2(
                h, y.reshape(n, -1), hy_w_out[i].astype(BF16), hy_b_out[i], norm_ffn_odd[i],
                moe_router[i], tp)
            plan = route_plan(gate_t, kind_t, cnt_t, b, tp, MOE_TILE)
            y2 = moe_sparse(xp, plan, moe_w_gate[i].astype(BF16), moe_w_up[i].astype(BF16),
                            moe_w_down[i].astype(BF16), tp)
            return moe_combine_final_norm(h.reshape(b, tp, d), y2, final_norm, seq)
    return final_rmsnorm(h.reshape(b, tp, d), final_norm, seq)
```

```python
import functools
import math

import jax
import jax.numpy as jnp
from jax import lax
from jax.experimental import pallas as pl
from jax.experimental.pallas import tpu as pltpu

F32 = jnp.float32
BF16 = jnp.bfloat16

D_MODEL = 1024
N_META = 16
EPS = 1e-5
D_SSD = 512
SSD_HEADDIM = 64
SSD_HEADS = 8
SSD_GROUPS = 2
SSD_STATE = 128
SSD_CONV = 5
CHUNK = 128
SSD_CONV_DIM = D_SSD + 2 * SSD_GROUPS * SSD_STATE
D_DIFF = 512
DIFF_HEADDIM = 64
DIFF_HEADS = 4
ROPE_THETA = 500000.0
ROPE_DIM = 16
HYENA_SHORT = 3
HYENA_BANDS = 16
HYENA_SHIFT = 0.05
HYENA_TARGET = 1e-2
HYENA_FAST_DECAY = 0.3
HYENA_SLOW_DECAY = 1.5
N_EXPERTS = 8
TOP_K = 2

LANES = 128
PAD = CHUNK - N_META
VMEM_LIMIT = 48 * 1024 * 1024
MOE_TILE = 512
SUB_ROWS = 64


def _cparams(n_axes):
    return pltpu.CompilerParams(dimension_semantics=("arbitrary",) * n_axes,
                                vmem_limit_bytes=VMEM_LIMIT)


def _resident(shape):
    zeros = (0,) * len(shape)
    return pl.BlockSpec(shape, lambda *_: zeros, pipeline_mode=pl.Buffered(1))


def _row_tile(n_rows, preferred):
    assert n_rows % CHUNK == 0
    tm = preferred
    while n_rows % tm:
        tm //= 2
    return tm


def _dot(a, b):
    return jnp.dot(a, b, preferred_element_type=F32)


def _dot_nt(a, b):
    return lax.dot_general(a, b, (((1,), (1,)), ((), ())), preferred_element_type=F32)


def _split3(x):
    hi = x.astype(BF16)
    r = x - hi.astype(F32)
    mid = r.astype(BF16)
    lo = (r - mid.astype(F32)).astype(BF16)
    return hi, mid, lo


def _dot3_right(x, m):
    hi, mid, lo = _split3(x)
    return _dot(hi, m) + _dot(mid, m) + _dot(lo, m)


def _dot3_left(m, x):
    hi, mid, lo = _split3(x)
    return _dot(m, hi) + _dot(m, mid) + _dot(m, lo)


def _rms(x, w):
    return x * lax.rsqrt(jnp.mean(x * x, axis=-1, keepdims=True) + EPS) * w


def _silu(x):
    return x * jax.nn.sigmoid(x)


def _rms_proj_kernel(*refs, n_out, has_bias, col_chunk):
    x_ref, g_ref = refs[:2]
    w_refs = refs[2:2 + n_out]
    b_refs = refs[2 + n_out:2 + 2 * n_out] if has_bias else (None,) * n_out
    o_refs = refs[-n_out:]
    xn = _rms(x_ref[...], g_ref[...]).astype(BF16)
    for w_ref, b_ref, o_ref in zip(w_refs, b_refs, o_refs):
        n = o_ref.shape[-1]
        for c0 in range(0, n, col_chunk):
            c1 = min(n, c0 + col_chunk)
            y = _dot(xn, w_ref[:, c0:c1])
            if b_ref is not None:
                y = y + b_ref[:, c0:c1]
            o_ref[:, c0:c1] = y.astype(o_ref.dtype)


def rms_proj(x, g, ws, bs, out_dtypes, tm=1024):
    n_rows, d = x.shape
    tm = _row_tile(n_rows, tm)
    n_out = len(ws)
    has_bias = bs is not None
    in_specs = [pl.BlockSpec((tm, d), lambda i: (i, 0)), _resident((1, d))]
    in_specs += [_resident(w.shape) for w in ws]
    args = [x, g.reshape(1, d)] + list(ws)
    if has_bias:
        in_specs += [_resident(b.shape) for b in bs]
        args += list(bs)
    out_shape = [jax.ShapeDtypeStruct((n_rows, w.shape[1]), dt) for w, dt in zip(ws, out_dtypes)]
    out_specs = [pl.BlockSpec((tm, w.shape[1]), lambda i: (i, 0)) for w in ws]
    return pl.pallas_call(
        functools.partial(_rms_proj_kernel, n_out=n_out, has_bias=has_bias, col_chunk=512),
        grid=(n_rows // tm,), in_specs=in_specs, out_specs=out_specs, out_shape=out_shape,
        compiler_params=_cparams(1), name="rms_proj")(*args)


def _ffn_kernel(*refs, n_in, tf):
    x_ref = refs[0]
    a_refs = refs[1:1 + n_in]
    wo_refs = refs[1 + n_in:1 + 2 * n_in]
    g_ref, wg_ref, wu_ref, wd_ref, o_ref, hm_ref = refs[1 + 2 * n_in:]
    x = x_ref[...]
    for a_ref, wo_ref in zip(a_refs, wo_refs):
        x = x + _dot(a_ref[...], wo_ref[...])
    xn = _rms(x, g_ref[...]).astype(BF16)
    f = wg_ref.shape[1]
    for c0 in range(0, f, tf):
        gt = _dot(xn, wg_ref[:, c0:c0 + tf])
        ut = _dot(xn, wu_ref[:, c0:c0 + tf])
        hm_ref[:, c0:c0 + tf] = (_silu(gt) * ut).astype(BF16)
    o_ref[...] = x + _dot(hm_ref[...], wd_ref[...])


def ffn_residual(x, a_list, wo_list, g, wg, wu, wd, tm=512, tf=256):
    n_rows, d = x.shape
    tm = _row_tile(n_rows, tm)
    f = wg.shape[1]
    row = lambda w: pl.BlockSpec((tm, w), lambda i: (i, 0))
    return pl.pallas_call(
        functools.partial(_ffn_kernel, n_in=len(a_list), tf=tf),
        grid=(n_rows // tm,),
        in_specs=[row(d)] + [row(a.shape[1]) for a in a_list] + [_resident(w.shape) for w in wo_list]
        + [_resident((1, d)), _resident(wg.shape), _resident(wu.shape), _resident(wd.shape)],
        out_specs=row(d),
        out_shape=jax.ShapeDtypeStruct((n_rows, d), F32),
        scratch_shapes=[pltpu.VMEM((tm, f), BF16)],
        compiler_params=_cparams(1), name="ffn")(
            x, *a_list, *wo_list, g.reshape(1, d), wg, wu, wd)


def _final_norm_kernel(x_ref, g_ref, o_ref):
    seq = o_ref.shape[1]
    for r0 in range(0, seq, 512):
        o_ref[0, r0:r0 + 512, :] = _rms(x_ref[0, CHUNK + r0:CHUNK + r0 + 512, :], g_ref[...])


def final_rmsnorm(h3, g, seq):
    b, tp, d = h3.shape
    assert seq % 512 == 0 and tp == seq + CHUNK
    return pl.pallas_call(
        _final_norm_kernel, grid=(b,),
        in_specs=[pl.BlockSpec((1, tp, d), lambda i: (i, 0, 0)), _resident((1, d))],
        out_specs=pl.BlockSpec((1, seq, d), lambda i: (i, 0, 0)),
        out_shape=jax.ShapeDtypeStruct((b, seq, d), F32),
        compiler_params=_cparams(1), name="final_norm")(h3, g.reshape(1, d))


def _rope_table_kernel(freq_ref, cos_ref, sin_ref):
    tp = cos_ref.shape[0]
    pos = (lax.broadcasted_iota(jnp.int32, (tp, LANES), 0) - PAD).astype(F32)
    lane = lax.broadcasted_iota(jnp.int32, (tp, LANES), 1) % DIFF_HEADDIM
    ang = pos * freq_ref[...]
    half = ROPE_DIM // 2
    rot = lane < ROPE_DIM
    cos_ref[...] = jnp.where(rot, jnp.cos(ang), 1.0)
    sin_ref[...] = jnp.where(lane < half, -jnp.sin(ang), jnp.where(rot, jnp.sin(ang), 0.0))


def rope_tables(tp):
    inv_freq = ROPE_THETA ** (-jnp.arange(0, ROPE_DIM, 2, dtype=F32) / ROPE_DIM)
    per_sub = jnp.concatenate([inv_freq, inv_freq, jnp.zeros((DIFF_HEADDIM - ROPE_DIM,), F32)])
    freq = jnp.concatenate([per_sub, per_sub]).reshape(1, LANES)
    return pl.pallas_call(
        _rope_table_kernel, grid=(1,),
        in_specs=[_resident((1, LANES))],
        out_specs=[pl.BlockSpec((tp, LANES), lambda i: (0, 0))] * 2,
        out_shape=[jax.ShapeDtypeStruct((tp, LANES), F32)] * 2,
        compiler_params=_cparams(1), name="rope_tables")(freq)


def _swap_matrix():
    src = lax.broadcasted_iota(jnp.int32, (LANES, LANES), 0)
    dst = lax.broadcasted_iota(jnp.int32, (LANES, LANES), 1)
    sub = dst % DIFF_HEADDIM
    half = ROPE_DIM // 2
    partner = jnp.where(sub < half, dst + half, jnp.where(sub < ROPE_DIM, dst - half, dst))
    return jnp.where(src == partner, 1.0, 0.0).astype(BF16)


def _rotate(x_bf16, cos, sin_signed, swap):
    return x_bf16.astype(F32) * cos + _dot(x_bf16, swap) * sin_signed


def _diff_attn_kernel(q_ref, k_ref, v_ref, cos_ref, sin_ref, lam_ref, w_ref, o_ref, kr_ref, s_ref,
                      *, lambda_init, tq):
    tp = k_ref.shape[1]
    swap = _swap_matrix()
    kr_ref[...] = _rotate(k_ref[0], cos_ref[...], sin_ref[...], swap).astype(BF16)
    scale = DIFF_HEADDIM ** -0.5 * math.log2(math.e)
    sub_head = lax.broadcasted_iota(jnp.int32, (tq, LANES), 1) // DIFF_HEADDIM
    real_key = lax.broadcasted_iota(jnp.int32, (tq, CHUNK), 1) >= PAD
    lp = lam_ref[...]
    lam = (jnp.exp(jnp.sum(lp[0:1] * lp[1:2], axis=-1, keepdims=True))
           - jnp.exp(jnp.sum(lp[2:3] * lp[3:4], axis=-1, keepdims=True)) + lambda_init)

    def stage_scores(qi, slot):
        rows = slice(qi * tq, (qi + 1) * tq)
        qr = (_rotate(q_ref[0, rows, :], cos_ref[rows, :], sin_ref[rows, :], swap) * scale).astype(BF16)
        for m in range(2):
            qm = jnp.where(sub_head == m, qr, jnp.zeros_like(qr))
            s_ref[slot, m] = _dot_nt(qm, kr_ref[...])

    def stage_finish(qi, slot):
        rows = slice(qi * tq, (qi + 1) * tq)
        maps = []
        for m in range(2):
            sa = jnp.where(real_key, s_ref[slot, m, :, 0:CHUNK], -jnp.inf)
            sb = s_ref[slot, m, :, CHUNK:]
            mx = jnp.maximum(jnp.max(sa, axis=-1, keepdims=True), jnp.max(sb, axis=-1, keepdims=True))
            pa = jnp.exp2(sa - mx)
            pb = jnp.exp2(sb - mx)
            den = jnp.sum(pa, axis=-1, keepdims=True) + jnp.sum(pb, axis=-1, keepdims=True)
            pv = _dot(pa.astype(BF16), v_ref[0, 0:CHUNK, :]) + _dot(pb.astype(BF16), v_ref[0, CHUNK:, :])
            maps.append(pv * (1.0 / den))
        o = maps[0] - lam * maps[1]
        o_ref[0, rows, :] = (_rms(o, w_ref[...]) * (1.0 - lambda_init)).astype(o_ref.dtype)

    n_blocks = tp // tq
    stage_scores(0, 0)
    for qi in range(n_blocks):
        if qi + 1 < n_blocks:
            stage_scores(qi + 1, (qi + 1) % 2)
        stage_finish(qi, qi % 2)


def diff_attention(qkv3, cos, sin, lam_params, subln_w, lambda_init, tq=544):
    b, tp, _ = qkv3.shape
    hw = 2 * DIFF_HEADDIM
    assert hw == LANES and tp % tq == 0 and tq % 16 == 0
    head = lambda k: pl.BlockSpec((1, tp, hw), lambda i, h: (i, 0, k * DIFF_HEADS + h))
    return pl.pallas_call(
        functools.partial(_diff_attn_kernel, lambda_init=lambda_init, tq=tq),
        grid=(b, DIFF_HEADS),
        in_specs=[head(0), head(1), head(2), _resident((tp, LANES)), _resident((tp, LANES)),
                  _resident(lam_params.shape), _resident((1, hw))],
        out_specs=head(0),
        out_shape=jax.ShapeDtypeStruct((b, tp, D_DIFF), BF16),
        scratch_shapes=[pltpu.VMEM((tp, hw), BF16), pltpu.VMEM((2, 2, tq, tp), F32)],
        compiler_params=_cparams(2), name="diff_attention")(
            qkv3, qkv3, qkv3, cos, sin, lam_params, subln_w.reshape(1, hw))


def _softplus(x):
    return jnp.maximum(x, 0.0) + jnp.log1p(jnp.exp(-jnp.abs(x)))


def _ssd_kernel(zx_ref, dt_ref, cw_ref, cb_ref, dtb_ref, alog_ref, aexp_ref, dsk_ref, nw_ref,
                o_ref, xact_ref, yacc_ref, dtv_ref, st_ref):
    tp = zx_ref.shape[1]
    n_chunks = tp // CHUNK
    gn = SSD_GROUPS * SSD_STATE

    valid = lax.broadcasted_iota(jnp.int32, (tp, LANES), 0) >= PAD
    dsum = dsk_ref[0:1, :] + dsk_ref[1:2, :]
    for c0 in range(0, SSD_CONV_DIM, LANES):
        x = jnp.where(valid, zx_ref[0, :, D_SSD + c0:D_SSD + c0 + LANES].astype(F32), 0.0)
        acc = cb_ref[:, c0:c0 + LANES] + x * cw_ref[SSD_CONV // 2:SSD_CONV // 2 + 1, c0:c0 + LANES]
        for k in range(SSD_CONV):
            off = k - SSD_CONV // 2
            if off != 0:
                acc = acc + pltpu.roll(x, (-off) % tp, axis=0) * cw_ref[k:k + 1, c0:c0 + LANES]
        act = _silu(acc)
        xact_ref[:, c0:c0 + LANES] = act.astype(BF16)
        if c0 < D_SSD:
            yacc_ref[:, c0:c0 + LANES] = act * dsum[:, c0:c0 + LANES]
    dtv_ref[...] = jnp.where(valid, _softplus(dt_ref[0] + dtb_ref[...]), 0.0)
    st_ref[...] = jnp.zeros_like(st_ref)

    ii = lax.broadcasted_iota(jnp.int32, (CHUNK, CHUNK), 0)
    jj = lax.broadcasted_iota(jnp.int32, (CHUNK, CHUNK), 1)
    lower = jj <= ii
    upper = jj >= ii
    tril = jnp.where(lower, 1.0, 0.0).astype(BF16)
    triu = jnp.where(upper, 1.0, 0.0).astype(BF16)
    a_row = -jnp.exp(alog_ref[...])
    src_lane = lax.broadcasted_iota(jnp.int32, (LANES, D_SSD), 0)
    dst_head = lax.broadcasted_iota(jnp.int32, (LANES, D_SSD), 1) // SSD_HEADDIM
    half = lax.broadcasted_iota(jnp.int32, (CHUNK, LANES), 1) // SSD_HEADDIM

    def chunk_step(c, d):
        rows = pl.ds(pl.multiple_of(c * CHUNK, CHUNK), CHUNK)
        tri_col, tri_row, mask = (tril, triu, lower) if d == 0 else (triu, tril, upper)
        expand = jnp.where(src_lane == dst_head + SSD_HEADS * d, 1.0, 0.0).astype(BF16)
        dtc = dtv_ref[rows, :]
        a = dtc * a_row
        dt_exp = _dot3_right(dtc, expand)
        a_exp = dt_exp * (-jnp.exp(aexp_ref[d:d + 1, :]))
        col = _dot3_left(tri_col, a_exp)
        rowv = _dot3_right(a.T, tri_row)
        tot = col[CHUNK - 1:CHUNK, :] if d == 0 else col[0:1, :]
        ecol = jnp.exp(col)
        wst = jnp.exp(tot - col)
        dec = jnp.exp(tot)
        xdt = xact_ref[rows, 0:D_SSD].astype(F32) * dt_exp
        xdt_b = xdt.astype(BF16)
        xw = (xdt * wst).astype(BF16)
        for g in range(SSD_GROUPS):
            bg = xact_ref[rows, D_SSD + g * SSD_STATE:D_SSD + (g + 1) * SSD_STATE]
            cg = xact_ref[rows, D_SSD + gn + g * SSD_STATE:D_SSD + gn + (g + 1) * SSD_STATE]
            cb = _dot_nt(cg, bg)
            bt = bg.astype(F32).T.astype(BF16)
            for pp in range(2):
                p = 2 * g + pp
                lanes = slice(p * LANES, (p + 1) * LANES)
                s_prev = st_ref[d, p]
                y = _dot(cg, s_prev.astype(BF16)) * ecol[:, lanes]
                for hh in range(2):
                    h = 2 * p + hh
                    ccol = col[:, h * SSD_HEADDIM:h * SSD_HEADDIM + 1]
                    crow = rowv[h + SSD_HEADS * d:h + SSD_HEADS * d + 1, :]
                    lmat = jnp.exp(jnp.where(mask, ccol - crow, -jnp.inf))
                    rhs = jnp.where(half == hh, xdt_b[:, lanes], jnp.zeros((CHUNK, LANES), BF16))
                    y = y + _dot((cb * lmat).astype(BF16), rhs)
                yacc_ref[rows, lanes] += y
                st_ref[d, p] = s_prev * dec[:, lanes] + _dot(bt, xw[:, lanes])

    def scan_body(i, carry):
        chunk_step(i, 0)
        chunk_step(n_chunks - 1 - i, 1)
        return carry

    lax.fori_loop(0, n_chunks, scan_body, 0, unroll=2)

    gw = D_SSD // SSD_GROUPS

    def gate_body(c, carry):
        rows = pl.ds(pl.multiple_of(c * CHUNK, CHUNK), CHUNK)
        yz = yacc_ref[rows, :] * _silu(zx_ref[0, rows, 0:D_SSD].astype(F32))
        for g in range(SSD_GROUPS):
            seg = _rms(yz[:, g * gw:(g + 1) * gw], nw_ref[:, g * gw:(g + 1) * gw])
            o_ref[0, rows, g * gw:(g + 1) * gw] = seg.astype(o_ref.dtype)
        return carry

    lax.fori_loop(0, n_chunks, gate_body, 0)


def ssd_mixer(zx3, dt3, conv_w, conv_b, a_log, dt_bias, d_skip, norm_w):
    b, tp, wzx = zx3.shape
    lane_pad = LANES - 2 * SSD_HEADS
    dtb = jnp.pad(dt_bias.reshape(1, -1), ((0, 0), (0, lane_pad)))
    alog = jnp.pad(a_log.reshape(1, -1), ((0, 0), (0, lane_pad)))
    aexp = jnp.repeat(a_log, SSD_HEADDIM, axis=1)
    dsk = jnp.repeat(d_skip, SSD_HEADDIM, axis=1)
    return pl.pallas_call(
        _ssd_kernel, grid=(b,),
        in_specs=[pl.BlockSpec((1, tp, wzx), lambda i: (i, 0, 0)),
                  pl.BlockSpec((1, tp, LANES), lambda i: (i, 0, 0)),
                  _resident(conv_w.shape), _resident((1, SSD_CONV_DIM)),
                  _resident((1, LANES)), _resident((1, LANES)),
                  _resident((2, D_SSD)), _resident((2, D_SSD)), _resident((1, D_SSD))],
        out_specs=pl.BlockSpec((1, tp, D_SSD), lambda i: (i, 0, 0)),
        out_shape=jax.ShapeDtypeStruct((b, tp, D_SSD), BF16),
        scratch_shapes=[pltpu.VMEM((tp, SSD_CONV_DIM), BF16), pltpu.VMEM((tp, D_SSD), F32),
                        pltpu.VMEM((tp, LANES), F32),
                        pltpu.VMEM((2, SSD_HEADS // 2, SSD_STATE, LANES), F32)],
        compiler_params=_cparams(1), name="ssd_mixer")(
            zx3, dt3, conv_w, conv_b.reshape(1, -1), dtb, alog, aexp, dsk, norm_w.reshape(1, -1))


def _hy_conv_kernel(p0_ref, p1_ref, p2_ref, w0_ref, w1_ref, w2_ref, b0_ref, b1_ref, b2_ref,
                    x0_ref, v_ref):
    tp = p0_ref.shape[1]
    ct = p0_ref.shape[2]
    valid = lax.broadcasted_iota(jnp.int32, (tp, LANES), 0) >= PAD

    def conv(p_ref, w_ref, b_ref, c0):
        x = jnp.where(valid, p_ref[0, :, c0:c0 + LANES].astype(F32), 0.0)
        acc = b_ref[:, c0:c0 + LANES] + x * w_ref[HYENA_SHORT // 2:HYENA_SHORT // 2 + 1, c0:c0 + LANES]
        for k in range(HYENA_SHORT):
            off = k - HYENA_SHORT // 2
            if off != 0:
                acc = acc + pltpu.roll(x, (-off) % tp, axis=0) * w_ref[k:k + 1, c0:c0 + LANES]
        return acc

    for c0 in range(0, ct, LANES):
        x0_ref[0, :, c0:c0 + LANES] = conv(p0_ref, w0_ref, b0_ref, c0).astype(x0_ref.dtype)
        v = conv(p2_ref, w2_ref, b2_ref, c0) * conv(p1_ref, w1_ref, b1_ref, c0)
        v_ref[0, :, c0:c0 + LANES] = jnp.where(valid, v, 0.0).astype(v_ref.dtype)


def hyena_conv_gate(p3, conv_w, conv_b, ct=256):
    b, tp, d3 = p3.shape
    d = d3 // 3
    nct = d // ct
    conv_b = conv_b.reshape(1, d3)
    pspec = lambda k: pl.BlockSpec((1, tp, ct), lambda i, j: (i, 0, j + k * nct))
    wspec = lambda k: pl.BlockSpec((HYENA_SHORT, ct), lambda i, j: (0, j + k * nct))
    bspec = lambda k: pl.BlockSpec((1, ct), lambda i, j: (0, j + k * nct))
    ospec = pl.BlockSpec((1, tp, ct), lambda i, j: (i, 0, j))
    return pl.pallas_call(
        _hy_conv_kernel, grid=(b, nct),
        in_specs=[pspec(0), pspec(1), pspec(2), wspec(0), wspec(1), wspec(2),
                  bspec(0), bspec(1), bspec(2)],
        out_specs=[ospec, ospec],
        out_shape=[jax.ShapeDtypeStruct((b, tp, d), BF16)] * 2,
        compiler_params=_cparams(2), name="hyena_conv_gate")(
            p3, p3, p3, conv_w, conv_w, conv_w, conv_b, conv_b, conv_b)


def _dot_f32(a, b):
    return jnp.dot(a, b, precision=lax.Precision.HIGHEST, preferred_element_type=F32)


def _hy_filter_kernel(w1t_ref, w1c_ref, w1s_ref, b1_ref, w2_ref, b2_ref, w3_ref, b3_ref, fr_ref,
                      w4f_ref, w4b_ref, bands_ref, deltas_ref, hf_ref, hb_ref, *, t_len):
    tp = hf_ref.shape[0]
    row = lax.broadcasted_iota(jnp.int32, (tp, 1), 0)
    pos = row.astype(F32)
    t = pos / (t_len - 1)
    ang = (2.0 * math.pi * pos / t_len) * bands_ref[...]
    h = t * w1t_ref[...] + _dot_f32(jnp.cos(ang), w1c_ref[...]) + _dot_f32(-jnp.sin(ang), w1s_ref[...])
    h = jnp.sin(fr_ref[0:1, :] * (h + b1_ref[...]))
    h = jnp.sin(fr_ref[1:2, :] * (_dot_f32(h, w2_ref[...]) + b2_ref[...]))
    h = jnp.sin(fr_ref[2:3, :] * (_dot_f32(h, w3_ref[...]) + b3_ref[...]))
    decay = jnp.exp(-t * deltas_ref[...]) + HYENA_SHIFT
    hf = jnp.where(row < t_len, _dot_f32(h, w4f_ref[...]) * decay, 0.0)
    hb = jnp.where((row >= 1) & (row < t_len), _dot_f32(h, w4b_ref[...]) * decay, 0.0)
    norm = (jnp.sum(jnp.abs(hf), axis=0, keepdims=True)
            + jnp.sum(jnp.abs(hb), axis=0, keepdims=True))
    hf_ref[...] = hf / norm
    hb_ref[...] = hb / norm


def hyena_filters(tp, t_len, w1, b1, w2, b2, w3, b3, w4, freq, ct=256):
    d = w4.shape[1] // 2
    fw = w1.shape[1]
    nct = d // ct
    bands = jnp.linspace(1e-4, HYENA_BANDS - 1, HYENA_BANDS, dtype=F32).reshape(1, -1)
    max_decay = math.log(HYENA_TARGET) / HYENA_FAST_DECAY
    min_decay = math.log(HYENA_TARGET) / HYENA_SLOW_DECAY
    deltas = jnp.abs(jnp.linspace(min_decay, max_decay, d, dtype=F32)).reshape(1, -1)
    nb = HYENA_BANDS
    full = lambda a: pl.BlockSpec(a.shape, lambda j: (0,) * a.ndim)
    args = [w1[0:1], w1[1:1 + nb], w1[1 + nb:], b1.reshape(1, fw), w2, b2.reshape(1, fw),
            w3, b3.reshape(1, fw), freq]
    return pl.pallas_call(
        functools.partial(_hy_filter_kernel, t_len=t_len), grid=(nct,),
        in_specs=[full(a) for a in args] + [
            pl.BlockSpec((fw, ct), lambda j: (0, j)), pl.BlockSpec((fw, ct), lambda j: (0, j + nct)),
            full(bands), pl.BlockSpec((1, ct), lambda j: (0, j))],
        out_specs=[pl.BlockSpec((tp, ct), lambda j: (0, j))] * 2,
        out_shape=[jax.ShapeDtypeStruct((tp, d), F32)] * 2,
        compiler_params=_cparams(1), name="hyena_filters")(*args, w4, w4, bands, deltas)


def _dft_table_kernel(cs_ref, *, period):
    tr = cs_ref.shape[0]
    tp = cs_ref.shape[1] // 2
    a = lax.broadcasted_iota(jnp.int32, (tr, tp), 0) + pl.program_id(0) * tr
    b = lax.broadcasted_iota(jnp.int32, (tr, tp), 1)
    n = (2 * a + 1) * (2 * b + 1)
    q = jnp.floor(n.astype(F32) * (1.0 / period)).astype(jnp.int32)
    r = (n - q * period).astype(F32)
    ang = r * (2.0 * math.pi / period)
    cs_ref[:, :tp] = jnp.cos(ang).astype(cs_ref.dtype)
    cs_ref[:, tp:] = jnp.sin(ang).astype(cs_ref.dtype)


def dft_tables(tp, tr=272):
    return pl.pallas_call(
        functools.partial(_dft_table_kernel, period=8 * tp), grid=(tp // tr,),
        out_specs=pl.BlockSpec((tr, 2 * tp), lambda i: (i, 0)),
        out_shape=jax.ShapeDtypeStruct((tp, 2 * tp), BF16),
        compiler_params=_cparams(1), name="dft_tables")()


def _split2(x):
    hi = x.astype(BF16)
    return hi, (x - hi.astype(F32)).astype(BF16)


def _hy_spectrum_kernel(hf_ref, hb_ref, cs_ref, p_ref, q_ref, *, mc):
    tp = hf_ref.shape[0]
    big_l = 2 * tp
    scale = 2.0 / big_l
    hs_hi, hs_lo = _split2(hf_ref[...] + hb_ref[...])
    hd_hi, hd_lo = _split2(hf_ref[...] - hb_ref[...])
    for m0 in range(0, tp, mc):
        rows = slice(m0, m0 + mc)
        phi = ((lax.broadcasted_iota(jnp.int32, (mc, 1), 0) + m0).astype(F32) + 0.5) * (math.pi / big_l)
        cphi, sphi = jnp.cos(phi), jnp.sin(phi)
        cm, sm = cs_ref[rows, :tp], cs_ref[rows, tp:]
        c_hs = _dot(cm, hs_hi) + _dot(cm, hs_lo)
        s_hs = _dot(sm, hs_hi) + _dot(sm, hs_lo)
        c_hd = _dot(cm, hd_hi) + _dot(cm, hd_lo)
        s_hd = _dot(sm, hd_hi) + _dot(sm, hd_lo)
        p_ref[rows, :] = (c_hs * cphi + s_hs * sphi) * scale
        q_ref[rows, :] = (s_hd * cphi - c_hd * sphi) * scale


def hyena_spectrum(hf, hb, cs, ct=256, mc=272):
    tp, d = hf.shape
    cspec = pl.BlockSpec((tp, ct), lambda j: (0, j))
    return pl.pallas_call(
        functools.partial(_hy_spectrum_kernel, mc=mc), grid=(d // ct,),
        in_specs=[cspec, cspec, _resident(cs.shape)],
        out_specs=[cspec, cspec],
        out_shape=[jax.ShapeDtypeStruct((tp, d), F32)] * 2,
        compiler_params=_cparams(1), name="hyena_spectrum")(hf, hb, cs)


def _hy_longconv_kernel(v_ref, x0_ref, p_ref, q_ref, skip_ref, cs_ref, o_ref, ycs_ref, *, mc):
    tp = v_ref.shape[1]
    v = v_ref[0]
    for m0 in range(0, tp, mc):
        rows = slice(m0, m0 + mc)
        a = _dot(cs_ref[rows, :tp], v)
        b = _dot(cs_ref[rows, tp:], v)
        p, q = p_ref[rows, :], q_ref[rows, :]
        ycs_ref[m0:m0 + mc, :] = (a * p - b * q).astype(BF16)
        ycs_ref[tp + m0:tp + m0 + mc, :] = (a * q + b * p).astype(BF16)
    for m0 in range(0, tp, mc):
        rows = slice(m0, m0 + mc)
        y = _dot(cs_ref[rows, :], ycs_ref[...])
        y = (y + v_ref[0, rows, :].astype(F32) * skip_ref[...]) * x0_ref[0, rows, :].astype(F32)
        o_ref[0, rows, :] = y.astype(o_ref.dtype)


def hyena_longconv(v3, x03, p, q, skip, cs, ct=256, mc=272):
    b, tp, d = v3.shape
    bspec = pl.BlockSpec((1, tp, ct), lambda j, i: (i, 0, j))
    cspec = pl.BlockSpec((tp, ct), lambda j, i: (0, j))
    return pl.pallas_call(
        functools.partial(_hy_longconv_kernel, mc=mc), grid=(d // ct, b),
        in_specs=[bspec, bspec, cspec, cspec, pl.BlockSpec((1, ct), lambda j, i: (0, j)),
                  _resident(cs.shape)],
        out_specs=bspec,
        out_shape=jax.ShapeDtypeStruct((b, tp, d), BF16),
        scratch_shapes=[pltpu.VMEM((2 * tp, ct), BF16)],
        compiler_params=_cparams(2), name="hyena_longconv")(
            v3, x03, p, q, skip.reshape(1, d), cs)


HI_MASK = 0xFFFF0000


def _pack_halves(x):
    w = x.shape[1] // 2
    lo = pltpu.bitcast(x[:, :w].astype(BF16).astype(F32), jnp.uint32)
    hi = pltpu.bitcast(x[:, w:].astype(BF16).astype(F32), jnp.uint32)
    return (lo >> 16) | (hi & jnp.uint32(HI_MASK))


def _unpack_halves(w):
    return (pltpu.bitcast(w << 16, F32), pltpu.bitcast(w & jnp.uint32(HI_MASK), F32))


def _router_kernel(x_ref, a_ref, wo_ref, bo_ref, g_ref, wr_ref, valid_ref,
                   h_ref, xp_ref, gate_ref, kind_ref, cnt_ref, carry_ref):
    @pl.when(pl.program_id(0) == 0)
    def _():
        carry_ref[...] = jnp.zeros_like(carry_ref)

    h = x_ref[...] + _dot(a_ref[...], wo_ref[...]) + bo_ref[...]
    h_ref[...] = h
    xn = _rms(h, g_ref[...])
    xp_ref[...] = _pack_halves(xn)
    tm = xn.shape[0]
    lane = lax.broadcasted_iota(jnp.int32, (tm, LANES), 1)
    x_hi, x_lo = _split2(xn)
    w_hi, w_lo = _split2(wr_ref[...])
    logits = _dot(x_hi, w_hi) + _dot(x_lo, w_hi) + _dot(x_hi, w_lo)
    logits = jnp.where(lane < N_EXPERTS, logits, -jnp.inf)
    v1 = jnp.max(logits, axis=-1, keepdims=True)
    i1 = jnp.min(jnp.where(logits == v1, lane, LANES), axis=-1, keepdims=True)
    rest = jnp.where(lane == i1, -jnp.inf, logits)
    v2 = jnp.max(rest, axis=-1, keepdims=True)
    i2 = jnp.min(jnp.where(rest == v2, lane, LANES), axis=-1, keepdims=True)
    e2 = jnp.exp(v2 - v1)
    den = 1.0 + e2
    gates = jnp.where(lane == i1, 1.0 / den, 0.0) + jnp.where(lane == i2, e2 / den, 0.0)
    kind = jnp.where(lane == i1, 1.0, 0.0) + jnp.where(lane == i2, 2.0, 0.0)
    valid = valid_ref[...]
    gate_t = gates.T[0:N_EXPERTS, :] * valid
    kind_t = kind.T[0:N_EXPERTS, :] * valid
    gate_ref[...] = gate_t
    kind_ref[...] = kind_t
    member = jnp.where(kind_t > 0.0, 1.0, 0.0)
    earlier = (lax.broadcasted_iota(jnp.int32, (tm, tm), 0)
               <= lax.broadcasted_iota(jnp.int32, (tm, tm), 1))
    running = _dot(member.astype(BF16), jnp.where(earlier, 1.0, 0.0).astype(BF16))
    cnt_ref[...] = running + carry_ref[:, 0:1]
    carry_ref[...] = carry_ref[...] + jnp.sum(member, axis=-1, keepdims=True)


def route_top2(x, a, wo, bo, g, router, tp, tm=512):
    n_rows, d = x.shape
    tm = _row_tile(n_rows, tm)
    wr = jnp.pad(router, ((0, 0), (0, LANES - router.shape[1])))
    valid = ((jnp.arange(n_rows, dtype=jnp.int32) % tp) >= PAD).astype(F32).reshape(1, n_rows)
    row = lambda w: pl.BlockSpec((tm, w), lambda i: (i, 0))
    espec = pl.BlockSpec((N_EXPERTS, tm), lambda i: (0, i))
    eshape = jax.ShapeDtypeStruct((N_EXPERTS, n_rows), F32)
    return pl.pallas_call(
        _router_kernel, grid=(n_rows // tm,),
        in_specs=[row(d), row(a.shape[1]), _resident(wo.shape), _resident((1, d)),
                  _resident((1, d)), _resident(wr.shape), pl.BlockSpec((1, tm), lambda i: (0, i))],
        out_specs=[row(d), row(d // 2), espec, espec, espec],
        out_shape=[jax.ShapeDtypeStruct((n_rows, d), F32),
                   jax.ShapeDtypeStruct((n_rows, d // 2), jnp.uint32), eshape, eshape, eshape],
        scratch_shapes=[pltpu.VMEM((N_EXPERTS, LANES), F32)],
        compiler_params=_cparams(1), name="moe_router")(
            x, a, wo, bo.reshape(1, d), g.reshape(1, d), wr, valid)


def _route_invert_kernel(te_ref, q0_ref, lo_ref, hi_ref, cnt_ref, gate_ref, kind_ref, trash_ref,
                         tok_ref, slot_ref, gcol_ref, cols_ref, *, n_rows, n_win):
    j = pl.program_id(0)
    e = te_ref[j]
    tm = tok_ref.shape[2]
    out_lane = lax.broadcasted_iota(jnp.int32, (SUB_ROWS, LANES), 1)
    total = cnt_ref[e, :, n_rows - 1:n_rows]
    for s in range(tm // SUB_ROWS):
        first = q0_ref[j] + s * SUB_ROWS + 1
        want = (first + lax.broadcasted_iota(jnp.int32, (SUB_ROWS, 1), 0)).astype(F32)
        lo = lo_ref[j * (tm // SUB_ROWS) + s]

        def block(k, acc, want=want):
            below, weight, second = acc
            cols = pl.ds(pl.multiple_of(k * LANES, LANES), LANES)
            c = cnt_ref[e, :, cols]
            kd = kind_ref[e, :, cols]
            below = below + jnp.where(c < want, 1.0, 0.0)
            hit = jnp.where(kd > 0.0, c, -1.0) == want
            weight = weight + jnp.where(hit, gate_ref[e, :, cols], 0.0)
            second = second + jnp.where(hit, jnp.where(kd == 2.0, 1.0, 0.0), 0.0)
            return below, weight, second

        zero = jnp.zeros((SUB_ROWS, LANES), F32)
        below, weight, second = lax.fori_loop(lo, hi_ref[j * (tm // SUB_ROWS) + s], block,
                                              (zero, zero, zero))
        rows = slice(s * SUB_ROWS, (s + 1) * SUB_ROWS)
        gcol_ref[rows, :] = jnp.sum(weight, axis=-1, keepdims=True)
        tok_col = (lo * LANES).astype(F32) + jnp.sum(below, axis=-1, keepdims=True)
        cols_ref[rows, :] = jnp.where(out_lane == 0, tok_col, jnp.where(
            out_lane == 1, jnp.sum(second, axis=-1, keepdims=True), jnp.where(
                out_lane == 2, jnp.where(want <= total, 1.0, 0.0), 0.0)))
    t = cols_ref[...].T
    found = t[2:3, :] > 0.5
    tok = jnp.where(found, t[0:1, :], 0.0).astype(jnp.int32)
    plane = jnp.where(t[1:2, :] > 0.5, n_rows, 0)
    trash = trash_ref[:, pl.ds(pl.multiple_of((j % n_win) * tm, tm), tm)]
    tok_ref[0] = tok
    slot_ref[0] = jnp.where(found, plane + tok, trash)


def route_plan(gate_t, kind_t, cnt_t, b, tp, tm):
    n = b * tp
    n_tiles = (TOP_K * b * (tp - PAD)) // tm + N_EXPERTS
    n_blk = n // LANES
    counts = cnt_t[:, -1].astype(jnp.int32)
    tiles = (counts + tm - 1) // tm
    tile_end = jnp.cumsum(tiles)
    tile_id = jnp.arange(n_tiles, dtype=jnp.int32)
    te = jnp.minimum(jnp.sum(tile_id[:, None] >= tile_end[None, :], axis=1), N_EXPERTS - 1)
    te = te.astype(jnp.int32)
    q0 = (tile_id - (tile_end - tiles)[te]) * tm
    n_sub = tm // SUB_ROWS
    block_end = cnt_t[:, LANES - 1::LANES].astype(jnp.int32)[te]
    first = q0[:, None] + jnp.arange(n_sub, dtype=jnp.int32)[None, :] * SUB_ROWS + 1
    last = jnp.minimum(first + SUB_ROWS - 1, counts[te][:, None])
    lo = jnp.sum(block_end[:, None, :] < first[:, :, None], axis=2)
    hi = jnp.minimum(jnp.sum(block_end[:, None, :] < last[:, :, None], axis=2) + 1, n_blk)
    hi = jnp.maximum(hi, lo).reshape(-1)
    lo = lo.reshape(-1)
    n_pad_rows = TOP_K * b * PAD
    n_win = max(1, n_pad_rows // tm)
    p = jnp.arange(n_win * tm, dtype=jnp.int32) % n_pad_rows
    trash = ((p // (b * PAD)) * n + ((p % (b * PAD)) // PAD) * tp + p % PAD).reshape(1, n_win * tm)
    grid_spec = pltpu.PrefetchScalarGridSpec(
        num_scalar_prefetch=4, grid=(n_tiles,),
        in_specs=[_resident((N_EXPERTS, 1, n))] * 3 + [_resident(trash.shape)],
        out_specs=[pl.BlockSpec((1, 1, tm), lambda j, *_: (j, 0, 0)),
                   pl.BlockSpec((1, 1, tm), lambda j, *_: (j, 0, 0)),
                   pl.BlockSpec((tm, 1), lambda j, *_: (j, 0))],
        scratch_shapes=[pltpu.VMEM((tm, LANES), F32)])
    tok, slot, gate = pl.pallas_call(
        functools.partial(_route_invert_kernel, n_rows=n, n_win=n_win), grid_spec=grid_spec,
        out_shape=[jax.ShapeDtypeStruct((n_tiles, 1, tm), jnp.int32),
                   jax.ShapeDtypeStruct((n_tiles, 1, tm), jnp.int32),
                   jax.ShapeDtypeStruct((n_tiles * tm, 1), F32)],
        compiler_params=_cparams(1), name="route_invert")(
            te, q0.astype(jnp.int32), lo.astype(jnp.int32), hi.astype(jnp.int32),
            cnt_t.reshape(N_EXPERTS, 1, n), gate_t.reshape(N_EXPERTS, 1, n),
            kind_t.reshape(N_EXPERTS, 1, n), trash)
    trash_tile = trash[:, (n_win - 1) * tm:].reshape(1, 1, tm)
    return tok, slot, gate, te, trash_tile


def _moe_sparse_kernel(te_ref, tokc_ref, tok1_ref, slotp_ref, slotc_ref, gate_ref, xp_hbm,
                       wg_ref, wu_ref, wd_ref, out_hbm, gbuf, ybuf, hm_ref, zbuf, gsem, ssem, zsem,
                       *, tf, tp):
    j = pl.program_id(0)
    tm = gbuf.shape[1]
    cur = j % 2

    @pl.when(j == 0)
    def _():
        zbuf[...] = jnp.zeros_like(zbuf)
        n_seq = out_hbm.shape[0] // tp
        copies = [pltpu.make_async_copy(zbuf, out_hbm.at[pl.ds(s * tp, PAD)], zsem)
                  for s in range(n_seq)]
        for c in copies:
            c.start()
        for c in copies:
            c.wait()

    prev = 1 - cur
    last = pl.num_programs(0) - 1

    def gather_row(tok_ref, buf_slot, r):
        return pltpu.make_async_copy(xp_hbm.at[pl.ds(tok_ref[0, 0, r], 1)],
                                     gbuf.at[buf_slot, pl.ds(r, 1)], gsem.at[buf_slot])

    def scatter_row(dst_ref, buf_slot, r):
        return pltpu.make_async_copy(ybuf.at[buf_slot, pl.ds(r, 1)],
                                     out_hbm.at[pl.ds(dst_ref[0, 0, r], 1)], ssem.at[buf_slot])

    def wait_gather(buf_slot):
        pltpu.make_async_copy(xp_hbm.at[pl.ds(0, tm)], gbuf.at[buf_slot], gsem.at[buf_slot]).wait()

    def wait_scatter(buf_slot):
        pltpu.make_async_copy(ybuf.at[buf_slot], out_hbm.at[pl.ds(0, tm)], ssem.at[buf_slot]).wait()

    @pl.when(j == 0)
    def _():
        ybuf[1] = jnp.zeros((tm, ybuf.shape[2]), ybuf.dtype)

        def body(r, carry):
            gather_row(tokc_ref, 0, r).start()
            return carry
        lax.fori_loop(0, tm, body, 0, unroll=8)

    wait_gather(cur)
    lo, hi = _unpack_halves(gbuf[cur])
    x = jnp.concatenate([lo.astype(BF16), hi.astype(BF16)], axis=1)
    gate = gate_ref[...]
    f = wg_ref.shape[2]
    n_chunks = f // tf
    moves = ([functools.partial(gather_row, tok1_ref, prev, r) for r in range(tm)]
             + [functools.partial(scatter_row, slotp_ref, prev, r) for r in range(tm)])
    per_chunk = -(-len(moves) // n_chunks)
    for ci in range(n_chunks):
        c0 = ci * tf
        gt = _dot(x, wg_ref[0, :, c0:c0 + tf])
        ut = _dot(x, wu_ref[0, :, c0:c0 + tf])
        hm_ref[:, c0:c0 + tf] = (_silu(gt) * ut * gate).astype(BF16)
        for move in moves[ci * per_chunk:(ci + 1) * per_chunk]:
            move().start()
    y = _pack_halves(_dot(hm_ref[...], wd_ref[0]))

    @pl.when(j >= 1)
    def _():
        wait_scatter(cur)

    ybuf[cur] = y

    @pl.when(j == last)
    def _():
        def body(r, carry):
            scatter_row(slotc_ref, cur, r).start()
            return carry
        lax.fori_loop(0, tm, body, 0, unroll=8)
        wait_scatter(prev)
        wait_scatter(cur)
        wait_gather(prev)


def moe_sparse(xp, plan, wg, wu, wd, tp, tf=512):
    tok, slot, gate, tile_expert, trash_tile = plan
    n_tiles, _, tm = tok.shape
    n_rows, dh = xp.shape
    n_e, d, fe = wg.shape
    slot = jnp.concatenate([trash_tile, slot])
    smem_tile = lambda imap: pl.BlockSpec((1, 1, tm), imap, memory_space=pltpu.SMEM)
    wspec = lambda shape: pl.BlockSpec(shape, lambda j, te: (te[j], 0, 0),
                                       pipeline_mode=pl.Buffered(1))
    grid_spec = pltpu.PrefetchScalarGridSpec(
        num_scalar_prefetch=1, grid=(n_tiles,),
        in_specs=[smem_tile(lambda j, te: (j, 0, 0)),
                  smem_tile(lambda j, te: (jnp.minimum(j + 1, n_tiles - 1), 0, 0)),
                  smem_tile(lambda j, te: (j, 0, 0)),
                  smem_tile(lambda j, te: (j + 1, 0, 0)),
                  pl.BlockSpec((tm, 1), lambda j, te: (j, 0)),
                  pl.BlockSpec(memory_space=pl.ANY),
                  wspec((1, d, fe)), wspec((1, d, fe)), wspec((1, fe, d))],
        out_specs=pl.BlockSpec(memory_space=pl.ANY),
        scratch_shapes=[pltpu.VMEM((2, tm, dh), jnp.uint32), pltpu.VMEM((2, tm, dh), jnp.uint32),
                        pltpu.VMEM((tm, fe), BF16), pltpu.VMEM((PAD, dh), jnp.uint32),
                        pltpu.SemaphoreType.DMA((2,)), pltpu.SemaphoreType.DMA((2,)),
                        pltpu.SemaphoreType.DMA(())])
    return pl.pallas_call(
        functools.partial(_moe_sparse_kernel, tf=tf, tp=tp), grid_spec=grid_spec,
        out_shape=jax.ShapeDtypeStruct((TOP_K * n_rows, dh), jnp.uint32),
        compiler_params=_cparams(1), name="moe_sparse")(
            tile_expert, tok, tok, slot, slot, gate, xp, wg, wu, wd)


def _moe_combine_norm_kernel(h_ref, y0_ref, y1_ref, g_ref, o_ref):
    dh = y0_ref.shape[3]
    a_lo, a_hi = _unpack_halves(y0_ref[0, 0])
    b_lo, b_hi = _unpack_halves(y1_ref[0, 0])
    lo = h_ref[0, :, :dh] + a_lo + b_lo
    hi = h_ref[0, :, dh:] + a_hi + b_hi
    ms = (jnp.sum(lo * lo, axis=-1, keepdims=True)
          + jnp.sum(hi * hi, axis=-1, keepdims=True)) / (2 * dh)
    inv = lax.rsqrt(ms + EPS)
    o_ref[0, :, :dh] = lo * inv * g_ref[:, :dh]
    o_ref[0, :, dh:] = hi * inv * g_ref[:, dh:]


def moe_combine_final_norm(h3, y2, g, seq):
    b, tp, d = h3.shape
    assert seq % CHUNK == 0 and tp == seq + CHUNK
    y4 = y2.reshape(TOP_K, b, tp, d // 2)
    yspec = lambda k: pl.BlockSpec((1, 1, CHUNK, d // 2), lambda i, j: (k, i, j + 1, 0))
    return pl.pallas_call(
        _moe_combine_norm_kernel, grid=(b, seq // CHUNK),
        in_specs=[pl.BlockSpec((1, CHUNK, d), lambda i, j: (i, j + 1, 0)), yspec(0), yspec(1),
                  _resident((1, d))],
        out_specs=pl.BlockSpec((1, CHUNK, d), lambda i, j: (i, j, 0)),
        out_shape=jax.ShapeDtypeStruct((b, seq, d), F32),
        compiler_params=_cparams(2), name="moe_combine_norm")(h3, y4, y4, g.reshape(1, d))


def kernel(x, meta_tokens, norm_mix_even, w_in_ab, ssd_conv_w, ssd_conv_b, ssd_a_log,
           ssd_dt_bias, ssd_d, ssd_norm_w, diff_lambda, diff_subln_w, w_out_ab,
           norm_ffn_even, ffn_w_gate, ffn_w_up, ffn_w_down, norm_mix_odd, hy_w_in, hy_b_in,
           hy_conv_w, hy_conv_b, hy_f_w1, hy_f_b1, hy_f_w2, hy_f_b2, hy_f_w3, hy_f_b3,
           hy_f_w4, hy_f_freq, hy_skip, hy_w_out, hy_b_out, norm_ffn_odd, moe_router,
           moe_w_gate, moe_w_up, moe_w_down, final_norm):
    b, seq, d = x.shape
    assert d == D_MODEL and seq % 512 == 0
    tp = seq + CHUNK
    t_len = seq + N_META
    n = b * tp
    meta = jnp.broadcast_to(meta_tokens[None].astype(x.dtype), (b, N_META, d))
    h = jnp.concatenate([jnp.zeros((b, PAD, d), x.dtype), meta, x], axis=1).reshape(n, d)
    cos, sin = rope_tables(tp)
    depth = norm_mix_even.shape[0] + norm_mix_odd.shape[0]
    o1 = D_SSD + SSD_CONV_DIM
    o2 = o1 + 2 * SSD_HEADS
    for layer in range(depth):
        i = layer // 2
        if layer % 2 == 0:
            w = w_in_ab[i]
            w_dt = jnp.pad(w[:, o1:o2], ((0, 0), (0, LANES - 2 * SSD_HEADS)))
            zx, dt, qkv = rms_proj(
                h, norm_mix_even[i],
                [w[:, :o1].astype(BF16), w_dt.astype(BF16), w[:, o2:].astype(BF16)],
                None, [BF16, F32, BF16])
            ssd_out = ssd_mixer(zx.reshape(b, tp, -1), dt.reshape(b, tp, -1), ssd_conv_w[i],
                                ssd_conv_b[i], ssd_a_log[i], ssd_dt_bias[i], ssd_d[i],
                                ssd_norm_w[i])
            lambda_init = 0.8 - 0.6 * math.exp(-0.3 * layer)
            attn_out = diff_attention(qkv.reshape(b, tp, -1), cos, sin, diff_lambda[i],
                                      diff_subln_w[i], lambda_init)
            wo = w_out_ab[i].astype(BF16)
            h = ffn_residual(h, [ssd_out.reshape(n, -1), attn_out.reshape(n, -1)],
                             [wo[:D_SSD], wo[D_SSD:]], norm_ffn_even[i],
                             ffn_w_gate[i].astype(BF16), ffn_w_up[i].astype(BF16),
                             ffn_w_down[i].astype(BF16))
        else:
            (p,) = rms_proj(h, norm_mix_odd[i], [hy_w_in[i].astype(BF16)],
                            [hy_b_in[i].reshape(1, -1)], [BF16])
            x0, v = hyena_conv_gate(p.reshape(b, tp, -1), hy_conv_w[i], hy_conv_b[i])
            hf, hb = hyena_filters(tp, t_len, hy_f_w1[i], hy_f_b1[i], hy_f_w2[i], hy_f_b2[i],
                                   hy_f_w3[i], hy_f_b3[i], hy_f_w4[i], hy_f_freq[i])
            cs = dft_tables(tp)
            fp, fq = hyena_spectrum(hf, hb, cs)
            y = hyena_longconv(v, x0, fp, fq, hy_skip[i], cs)
            assert layer == depth - 1, "an expert layer must close the trunk"
            h, xp, gate_t, kind_t, cnt_t = route_top2(
                h, y.reshape(n, -1), hy_w_out[i].astype(BF16), hy_b_out[i], norm_ffn_odd[i],
                moe_router[i], tp)
            plan = route_plan(gate_t, kind_t, cnt_t, b, tp, MOE_TILE)
            y2 = moe_sparse(xp, plan, moe_w_gate[i].astype(BF16), moe_w_up[i].astype(BF16),
                            moe_w_down[i].astype(BF16), tp)
            return moe_combine_final_norm(h.reshape(b, tp, d), y2, final_norm, seq)
    return final_rmsnorm(h.reshape(b, tp, d), final_norm, seq)
```
